```python
import math
import jax, jax.numpy as jnp
from jax import lax
import numpy as np

D_MODEL = 1024
BATCH = 8
SEQ = 2048
DEPTH = 4
DEC_BATCH = 8
DEC_SEQ = 16
PAST_LEN = 4096

CHUNK = 64
N_A_LAYERS = DEPTH // 2
N_B_LAYERS = DEPTH - N_A_LAYERS
D_FF = 2816
SSM_GROUP = 16
N_GROUPS = D_MODEL // SSM_GROUP
SSM_STATE = 64
DT_MIN = 1e-3
DT_MAX = 1e-1
N_HEADS = 16
HEAD_DIM = D_MODEL // N_HEADS
N_LEFT_CHUNKS = 8
BAND_PAST = N_LEFT_CHUNKS * CHUNK
BAND = (N_LEFT_CHUNKS + 1) * CHUNK
MAX_REL = 128
N_REL = 2 * MAX_REL + 1
EPS = 1e-6
SCALE = HEAD_DIM ** -0.5
NEG_INF = -1e30

kernel_name = 'yoco_s5_chunkband_relbias_encoder_step'


def _rmsnorm(x, g):
    xf = x.astype(jnp.float32)
    y = xf * lax.rsqrt(jnp.mean(xf * xf, axis=-1, keepdims=True) + EPS)
    return (y * g.astype(jnp.float32)).astype(x.dtype)


def _swiglu(x, w_in, w_out):
    gate, up = jnp.split(x @ w_in, 2, axis=-1)
    return (jax.nn.silu(gate) * up) @ w_out


def _half_ffn(x, g_pre, g_post, w_in, w_out):
    return x + 0.5 * _rmsnorm(_swiglu(_rmsnorm(x, g_pre), w_in, w_out), g_post)


def _ssm_combine(e1, e2):
    a1, b1 = e1
    a2, b2 = e2
    return a1 * a2, a2 * b1 + b2


def _s5_mixer(u, h0_re, h0_im, lam_re, lam_im, log_step, b_re, b_im, c_re, c_im, d_skip, w_glu):
    f32 = jnp.float32
    n, l, _ = u.shape
    uf = u.astype(f32).reshape(n, l, N_GROUPS, SSM_GROUP)
    lam = lax.complex(lam_re.astype(f32), lam_im.astype(f32))
    step = jnp.exp(log_step.astype(f32))[:, None]
    a_bar = jnp.exp(lam * step)
    b_bar = ((a_bar - 1) / lam)[..., None] * lax.complex(b_re.astype(f32), b_im.astype(f32))
    bu = jnp.einsum('gpc,nlgc->nlgp', b_bar, uf.astype(jnp.complex64))
    h0 = lax.complex(h0_re.astype(f32), h0_im.astype(f32))
    bu = bu.at[:, 0].add(a_bar * h0)
    _, h = lax.associative_scan(_ssm_combine, (jnp.broadcast_to(a_bar, bu.shape), bu), axis=1)
    c = lax.complex(c_re.astype(f32), c_im.astype(f32))
    y = jnp.einsum('gcp,nlgp->nlgc', c, h).real + d_skip.astype(f32).reshape(N_GROUPS, SSM_GROUP) * uf
    y = jax.nn.gelu(y.reshape(n, l, D_MODEL))
    val, gate = jnp.split(y @ w_glu.astype(f32), 2, axis=-1)
    out = (val * jax.nn.sigmoid(gate)).astype(u.dtype)
    h_last = h[:, -1]
    return out, h_last.real.astype(h0_re.dtype), h_last.imag.astype(h0_im.dtype)


def _band_attention_prompt(xn, k, v, w_q, w_o, table):
    f32 = jnp.float32
    n, l, _ = xn.shape
    nc = l // CHUNK
    q = (xn @ w_q).astype(f32).reshape(n, nc, CHUNK, N_HEADS, HEAD_DIM)
    pad = ((0, 0), (BAND_PAST, 0), (0, 0), (0, 0))
    kp = jnp.pad(k, pad).reshape(n, nc + N_LEFT_CHUNKS, CHUNK, N_HEADS, HEAD_DIM)
    vp = jnp.pad(v, pad).reshape(n, nc + N_LEFT_CHUNKS, CHUNK, N_HEADS, HEAD_DIM)
    kb = jnp.concatenate([kp[:, o:o + nc] for o in range(N_LEFT_CHUNKS + 1)], axis=2)
    vb = jnp.concatenate([vp[:, o:o + nc] for o in range(N_LEFT_CHUNKS + 1)], axis=2)
    qi = jnp.arange(CHUNK)
    kj = jnp.arange(BAND)
    rel = jnp.clip(qi[:, None] + BAND_PAST - kj[None, :], -MAX_REL, MAX_REL) + MAX_REL
    bias = table.astype(f32)[:, rel]
    valid = (jnp.arange(nc)[:, None] - N_LEFT_CHUNKS + kj[None, :] // CHUNK) >= 0
    s = jnp.einsum('ncihd,ncjhd->nhcij', q, kb.astype(f32)) * SCALE + bias[:, None]
    s = jnp.where(valid[:, None, :], s, NEG_INF)
    p = jax.nn.softmax(s, axis=-1).astype(v.dtype)
    o = jnp.einsum('nhcij,ncjhd->ncihd', p, vb).reshape(n, l, D_MODEL)
    return o @ w_o


def _band_attention_sample(xn, k, v, q_pos, k_pos, w_q, w_o, table):
    f32 = jnp.float32
    n, s_len, _ = xn.shape
    q = (xn @ w_q).astype(f32).reshape(n, s_len, N_HEADS, HEAD_DIM)
    rel = jnp.clip(q_pos[:, None] - k_pos[None, :], -MAX_REL, MAX_REL) + MAX_REL
    bias = table.astype(f32)[:, rel]
    qc = q_pos // CHUNK
    kc = k_pos // CHUNK
    valid = (kc[None, :] <= qc[:, None]) & (kc[None, :] >= qc[:, None] - N_LEFT_CHUNKS)
    sc = jnp.einsum('nqhd,nkhd->nhqk', q, k.astype(f32)) * SCALE + bias
    sc = jnp.where(valid, sc, NEG_INF)
    p = jax.nn.softmax(sc, axis=-1).astype(v.dtype)
    o = jnp.einsum('nhqk,nkhd->nqhd', p, v).reshape(n, s_len, D_MODEL)
    return o @ w_o


def setup_inputs(seed: int = 0) -> dict:
    key = jax.random.key(seed)
    ks = jax.random.split(key, 24)
    f32 = jnp.float32
    D = D_MODEL

    def nrm(k, shape, scale):
        return scale * jax.random.normal(k, shape, f32)

    cache_rows = min(BAND_PAST, PAST_LEN)
    ssm_state_shape = (N_A_LAYERS, DEC_BATCH, N_GROUPS, SSM_STATE)
    return {
        'x_prompt': nrm(ks[0], (BATCH, SEQ, D), 1.0),
        'x_sample': nrm(ks[1], (DEC_BATCH, DEC_SEQ, D), 1.0),
        'state_ssm_re': nrm(ks[2], ssm_state_shape, 0.1),
        'state_ssm_im': nrm(ks[3], ssm_state_shape, 0.1),
        'cache_k': nrm(ks[4], (DEC_BATCH, cache_rows, N_HEADS, HEAD_DIM), 1.0),
        'cache_v': nrm(ks[5], (DEC_BATCH, cache_rows, N_HEADS, HEAD_DIM), 1.0),
        'norm_gains': 1.0 + nrm(ks[6], (DEPTH, 6, D), 0.05),
        'ffn1_w_in': nrm(ks[7], (DEPTH, D, 2 * D_FF), D ** -0.5),
        'ffn1_w_out': nrm(ks[8], (DEPTH, D_FF, D), D_FF ** -0.5),
        'ffn2_w_in': nrm(ks[9], (DEPTH, D, 2 * D_FF), D ** -0.5),
        'ffn2_w_out': nrm(ks[10], (DEPTH, D_FF, D), D_FF ** -0.5),
        'ssm_lambda_re': -0.5 * jnp.exp(nrm(ks[11], (N_A_LAYERS, N_GROUPS, SSM_STATE), 0.01)),
        'ssm_lambda_im': math.pi * jnp.broadcast_to(jnp.arange(SSM_STATE, dtype=f32), (N_A_LAYERS, N_GROUPS, SSM_STATE)),
        'ssm_log_step': jax.random.uniform(ks[12], (N_A_LAYERS, N_GROUPS), f32, math.log(DT_MIN), math.log(DT_MAX)),
        'ssm_b_re': nrm(ks[13], (N_A_LAYERS, N_GROUPS, SSM_STATE, SSM_GROUP), (2 * SSM_GROUP) ** -0.5),
        'ssm_b_im': nrm(ks[14], (N_A_LAYERS, N_GROUPS, SSM_STATE, SSM_GROUP), (2 * SSM_GROUP) ** -0.5),
        'ssm_c_re': nrm(ks[15], (N_A_LAYERS, N_GROUPS, SSM_GROUP, SSM_STATE), SSM_STATE ** -0.5),
        'ssm_c_im': nrm(ks[16], (N_A_LAYERS, N_GROUPS, SSM_GROUP, SSM_STATE), SSM_STATE ** -0.5),
        'ssm_d': nrm(ks[17], (N_A_LAYERS, D), 1.0),
        'ssm_w_glu': nrm(ks[18], (N_A_LAYERS, D, 2 * D), D ** -0.5),
        'kv_norm': 1.0 + nrm(ks[19], (D,), 0.05),
        'w_kv': nrm(ks[20], (D, 2 * D), D ** -0.5),
        'attn_w_q': nrm(ks[21], (N_B_LAYERS, D, D), D ** -0.5),
        'attn_w_o': nrm(ks[22], (N_B_LAYERS, D, D), D ** -0.5),
        'attn_rel_bias': nrm(ks[23], (N_B_LAYERS, N_HEADS, N_REL), 0.2),
    }


def reference(x_prompt, x_sample, state_ssm_re, state_ssm_im, cache_k, cache_v,
              norm_gains, ffn1_w_in, ffn1_w_out, ffn2_w_in, ffn2_w_out,
              ssm_lambda_re, ssm_lambda_im, ssm_log_step, ssm_b_re, ssm_b_im,
              ssm_c_re, ssm_c_im, ssm_d, ssm_w_glu,
              kv_norm, w_kv, attn_w_q, attn_w_o, attn_rel_bias):

    def trunk(x, h0_re, h0_im, past_k, past_v):
        n, l, _ = x.shape
        ssm_re, ssm_im = [], []
        k = v = None
        if past_k is not None:
            r = past_k.shape[1]
            q_pos = PAST_LEN + jnp.arange(l)
            k_pos = jnp.concatenate([PAST_LEN - r + jnp.arange(r), q_pos])
        for layer in range(DEPTH):
            g = norm_gains[layer]
            x = _half_ffn(x, g[0], g[1], ffn1_w_in[layer], ffn1_w_out[layer])
            xn = _rmsnorm(x, g[2])
            if layer < N_A_LAYERS:
                y, hr, hi = _s5_mixer(xn, h0_re[layer], h0_im[layer],
                                      ssm_lambda_re[layer], ssm_lambda_im[layer], ssm_log_step[layer],
                                      ssm_b_re[layer], ssm_b_im[layer], ssm_c_re[layer], ssm_c_im[layer],
                                      ssm_d[layer], ssm_w_glu[layer])
                ssm_re.append(hr)
                ssm_im.append(hi)
            else:
                b = layer - N_A_LAYERS
                if past_k is None:
                    y = _band_attention_prompt(xn, k, v, attn_w_q[b], attn_w_o[b], attn_rel_bias[b])
                else:
                    y = _band_attention_sample(xn, jnp.concatenate([past_k, k], axis=1),
                                               jnp.concatenate([past_v, v], axis=1),
                                               q_pos, k_pos, attn_w_q[b], attn_w_o[b], attn_rel_bias[b])
            x = x + _rmsnorm(y, g[3])
            x = _half_ffn(x, g[4], g[5], ffn2_w_in[layer], ffn2_w_out[layer])
            if layer == N_A_LAYERS - 1:
                k, v = jnp.split(_rmsnorm(x, kv_norm) @ w_kv, 2, axis=-1)
                k = k.reshape(n, l, N_HEADS, HEAD_DIM)
                v = v.reshape(n, l, N_HEADS, HEAD_DIM)
        return x, jnp.stack(ssm_re), jnp.stack(ssm_im), k, v

    bp, lp, _ = x_prompt.shape
    h0_zero = jnp.zeros((N_A_LAYERS, bp, N_GROUPS, SSM_STATE), state_ssm_re.dtype)
    y_prompt, ssm_re_p, ssm_im_p, k_p, v_p = trunk(x_prompt, h0_zero, h0_zero, None, None)
    keep = min(BAND_PAST, lp)
    k_prompt = k_p[:, lp - keep:]
    v_prompt = v_p[:, lp - keep:]

    y_sample, ssm_re_s, ssm_im_s, k_sample, v_sample = trunk(x_sample, state_ssm_re, state_ssm_im, cache_k, cache_v)

    return (y_prompt, y_sample, ssm_re_p, ssm_im_p, k_prompt, v_prompt, ssm_re_s, ssm_im_s, k_sample, v_sample)
```

```python
import functools
import math

import numpy as np
import jax
import jax.numpy as jnp
from jax import lax
from jax.experimental import pallas as pl
from jax.experimental.pallas import tpu as pltpu

D_MODEL = 1024
DEPTH = 4
PAST_LEN = 4096
CHUNK = 64
N_A_LAYERS = DEPTH // 2
D_FF = 2816
SSM_GROUP = 16
N_GROUPS = D_MODEL // SSM_GROUP
SSM_STATE = 64
N_HEADS = 16
HEAD_DIM = D_MODEL // N_HEADS
N_LEFT_CHUNKS = 8
BAND_PAST = N_LEFT_CHUNKS * CHUNK
BAND = (N_LEFT_CHUNKS + 1) * CHUNK
MAX_REL = 128
EPS = 1e-6
SCALE = HEAD_DIM ** -0.5
NEG_INF = -1e30

F32 = jnp.float32
BF16 = jnp.bfloat16

LANES = 128
SUBLANES = 8
VMEM_LIMIT = 56 * 1024 * 1024

FFN_ROWS = 512
FFN_CHUNK = 256
S5_STEPS = 64
S5_COLS = 1024
GROUPS_PER_BLOCK = LANES // SSM_GROUP
N_CBLOCKS = D_MODEL // LANES
STATE_COLS = N_GROUPS * SSM_STATE
BLOCK_STATE = GROUPS_PER_BLOCK * SSM_STATE
ATT_QCHUNKS = 2
HEAD_PAIRS = N_HEADS // 2


def _const_spec(shape):
    zeros = (0,) * len(shape)
    return pl.BlockSpec(shape, lambda *_: zeros, pipeline_mode=pl.Buffered(1))


def _rms(x, g):
    return x * lax.rsqrt(jnp.mean(x * x, axis=-1, keepdims=True) + EPS) * g


def _sigmoid(x):
    return 1.0 / (1.0 + jnp.exp(-x))


def _gelu_tanh(x):
    c = math.sqrt(2.0 / math.pi)
    return 0.5 * x * (1.0 + jnp.tanh(c * (x + 0.044715 * (x * x * x))))


def _dot(a, b):
    return jnp.dot(a, b, preferred_element_type=F32)


def _dot_nt(a, b):
    return lax.dot_general(a, b, (((1,), (1,)), ((), ())), preferred_element_type=F32)


def _ffn_kernel(x_ref, g_ref, w_in_ref, w_out_ref, o_ref, *, pre, post):
    x = x_ref[...]
    xn = _rms(x, g_ref[pre:pre + 1, :]).astype(BF16)
    acc = jnp.zeros(x.shape, F32)
    for c in range(D_FF // FFN_CHUNK):
        lo = c * FFN_CHUNK
        gate = _dot(xn, w_in_ref[:, lo:lo + FFN_CHUNK])
        up = _dot(xn, w_in_ref[:, D_FF + lo:D_FF + lo + FFN_CHUNK])
        act = (gate * _sigmoid(gate) * up).astype(BF16)
        acc = acc + _dot(act, w_out_ref[lo:lo + FFN_CHUNK, :])
    o_ref[...] = x + 0.5 * _rms(acc, g_ref[post:post + 1, :])


def _half_ffn(x, gains, w_in, w_out, pre, post):
    rows = x.shape[0]
    tm = min(FFN_ROWS, rows)
    return pl.pallas_call(
        functools.partial(_ffn_kernel, pre=pre, post=post),
        out_shape=jax.ShapeDtypeStruct(x.shape, F32),
        grid=(rows // tm,),
        in_specs=[
            pl.BlockSpec((tm, D_MODEL), lambda i: (i, 0)),
            _const_spec(gains.shape),
            _const_spec(w_in.shape),
            _const_spec(w_out.shape),
        ],
        out_specs=pl.BlockSpec((tm, D_MODEL), lambda i: (i, 0)),
        compiler_params=pltpu.CompilerParams(
            dimension_semantics=("arbitrary",), vmem_limit_bytes=VMEM_LIMIT),
        name="half_ffn",
    )(x, gains, w_in, w_out)


def _s5_kernel(x_ref, g_ref, bhh_ref, blo_ref, cre_ref, cim_ref, are_ref, aim_ref,
               h0re_ref, h0im_ref, d_ref, wglu_ref,
               o_ref, hre_out, him_out,
               bure, buim, hre, him, y_scr, *, steps, batch):
    i = pl.program_id(0)

    @pl.when(i == 0)
    def _():
        hre[...] = h0re_ref[...]
        him[...] = h0im_ref[...]

    x = x_ref[...]
    xn = _rms(x, g_ref[2:3, :])
    x_hi = xn.astype(BF16)
    x_lo = (xn - x_hi.astype(F32)).astype(BF16)

    for k in range(N_CBLOCKS):
        xh = x_hi[:, k * LANES:(k + 1) * LANES]
        xl = x_lo[:, k * LANES:(k + 1) * LANES]
        bu = _dot(jnp.concatenate([xh, xl], axis=1), bhh_ref[k]) + _dot(xh, blo_ref[k])
        bure[:, k * BLOCK_STATE:(k + 1) * BLOCK_STATE] = bu[:, :BLOCK_STATE]
        buim[:, k * BLOCK_STATE:(k + 1) * BLOCK_STATE] = bu[:, BLOCK_STATE:]

    for cb in range(STATE_COLS // S5_COLS):
        cols = slice(cb * S5_COLS, (cb + 1) * S5_COLS)
        ar = are_ref[:, cols]
        ai = aim_ref[:, cols]

        def step(t, carry, cols=cols, ar=ar, ai=ai):
            hr, hi = carry
            rows = pl.ds(pl.multiple_of(t * batch, batch), batch)
            nhr = ar * hr - ai * hi + bure[rows, cols]
            nhi = ar * hi + ai * hr + buim[rows, cols]
            bure[rows, cols] = nhr
            buim[rows, cols] = nhi
            return nhr, nhi

        hr, hi = lax.fori_loop(0, steps, step, (hre[:, cols], him[:, cols]), unroll=2)
        hre[:, cols] = hr
        him[:, cols] = hi

    hre_out[...] = hre[...]
    him_out[...] = him[...]

    for k in range(N_CBLOCKS):
        sc = slice(k * BLOCK_STATE, (k + 1) * BLOCK_STATE)
        yk = _dot(bure[:, sc].astype(BF16), cre_ref[k]) + _dot(buim[:, sc].astype(BF16), cim_ref[k])
        y_scr[:, k * LANES:(k + 1) * LANES] = yk
    y = _gelu_tanh(y_scr[...] + d_ref[...] * xn)
    z = _dot(y.astype(BF16), wglu_ref[...])
    out = z[:, :D_MODEL] * _sigmoid(z[:, D_MODEL:])
    o_ref[...] = x + _rms(out, g_ref[3:4, :])


def _s5_layer(x, gains, p, h0re, h0im, batch):
    rows = x.shape[0]
    steps = min(S5_STEPS, rows // batch)
    tr = steps * batch
    consts = [gains, p["bhh"], p["blo"], p["cre"], p["cim"], p["are"], p["aim"],
              h0re, h0im, p["d"], p["wglu"]]
    state = jax.ShapeDtypeStruct((batch, STATE_COLS), F32)
    return pl.pallas_call(
        functools.partial(_s5_kernel, steps=steps, batch=batch),
        out_shape=(jax.ShapeDtypeStruct(x.shape, F32), state, state),
        grid=(rows // tr,),
        in_specs=[pl.BlockSpec((tr, D_MODEL), lambda i: (i, 0))] + [_const_spec(c.shape) for c in consts],
        out_specs=(pl.BlockSpec((tr, D_MODEL), lambda i: (i, 0)),
                   pl.BlockSpec((batch, STATE_COLS), lambda i: (0, 0)),
                   pl.BlockSpec((batch, STATE_COLS), lambda i: (0, 0))),
        scratch_shapes=[
            pltpu.VMEM((tr, STATE_COLS), F32),
            pltpu.VMEM((tr, STATE_COLS), F32),
            pltpu.VMEM((batch, STATE_COLS), F32),
            pltpu.VMEM((batch, STATE_COLS), F32),
            pltpu.VMEM((tr, D_MODEL), F32),
        ],
        compiler_params=pltpu.CompilerParams(
            dimension_semantics=("arbitrary",), vmem_limit_bytes=VMEM_LIMIT),
        name="s5_mixer",
    )(x, *consts)


def _s5_params(lam_re, lam_im, log_step, b_re, b_im, c_re, c_im, d_skip, w_glu, batch):
    step = jnp.exp(log_step.astype(F32))[:, None]
    lr = lam_re.astype(F32)
    li = lam_im.astype(F32)
    mag = jnp.exp(lr * step)
    a_re = mag * jnp.cos(li * step)
    a_im = mag * jnp.sin(li * step)
    den = lr * lr + li * li
    coef_re = ((a_re - 1.0) * lr + a_im * li) / den
    coef_im = (a_im * lr - (a_re - 1.0) * li) / den
    bre = b_re.astype(F32)
    bim = b_im.astype(F32)
    bb_re = coef_re[..., None] * bre - coef_im[..., None] * bim
    bb_im = coef_re[..., None] * bim + coef_im[..., None] * bre

    eye = jnp.eye(GROUPS_PER_BLOCK, dtype=F32)

    def pack_b(b):
        b4 = b.reshape(N_CBLOCKS, GROUPS_PER_BLOCK, SSM_STATE, SSM_GROUP).transpose(0, 1, 3, 2)
        full = b4[:, :, :, None, :] * eye[None, :, None, :, None]
        return full.reshape(N_CBLOCKS, LANES, BLOCK_STATE)

    def pack_c(c):
        c4 = c.reshape(N_CBLOCKS, GROUPS_PER_BLOCK, SSM_GROUP, SSM_STATE).transpose(0, 1, 3, 2)
        full = c4[:, :, :, None, :] * eye[None, :, None, :, None]
        return full.reshape(N_CBLOCKS, BLOCK_STATE, LANES)

    bcat = jnp.concatenate([pack_b(bb_re), pack_b(bb_im)], axis=-1)
    b_hi = bcat.astype(BF16)
    b_lo = (bcat - b_hi.astype(F32)).astype(BF16)
    return {
        "bhh": jnp.concatenate([b_hi, b_hi], axis=1),
        "blo": b_lo,
        "cre": pack_c(c_re.astype(F32)).astype(BF16),
        "cim": pack_c(-c_im.astype(F32)).astype(BF16),
        "are": jnp.broadcast_to(a_re.reshape(1, STATE_COLS), (batch, STATE_COLS)),
        "aim": jnp.broadcast_to(a_im.reshape(1, STATE_COLS), (batch, STATE_COLS)),
        "d": d_skip.astype(F32).reshape(1, D_MODEL),
        "wglu": w_glu.astype(BF16),
    }


def _kv_prompt_kernel(x_ref, g_ref, w_ref, kp_ref, vp_ref, k_ref, v_ref):
    i = pl.program_id(1)

    @pl.when(i == 0)
    def _():
        kp_ref[...] = jnp.zeros(kp_ref.shape, BF16)
        vp_ref[...] = jnp.zeros(vp_ref.shape, BF16)

    @pl.when(i > 0)
    def _():
        kv = _dot(_rms(x_ref[0], g_ref[...]).astype(BF16), w_ref[...])
        kp_ref[0] = kv[:, :D_MODEL].astype(BF16)
        vp_ref[0] = kv[:, D_MODEL:].astype(BF16)

        @pl.when(i == pl.num_programs(1) - 1)
        def _():
            k_ref[0] = kv[:, :D_MODEL]
            v_ref[0] = kv[:, D_MODEL:]


def _kv_prompt(x, g, w):
    n, l, _ = x.shape
    tiles = l // BAND_PAST
    padded = jax.ShapeDtypeStruct((n, BAND_PAST + l, D_MODEL), BF16)
    tail = jax.ShapeDtypeStruct((n, BAND_PAST, D_MODEL), F32)
    row_block = pl.BlockSpec((1, BAND_PAST, D_MODEL), lambda b, i: (b, i, 0))
    tail_block = pl.BlockSpec((1, BAND_PAST, D_MODEL), lambda b, i: (b, 0, 0))
    return pl.pallas_call(
        _kv_prompt_kernel,
        out_shape=(padded, padded, tail, tail),
        grid=(n, tiles + 1),
        in_specs=[
            pl.BlockSpec((1, BAND_PAST, D_MODEL), lambda b, i: (b, jnp.maximum(i - 1, 0), 0)),
            _const_spec(g.shape),
            _const_spec(w.shape),
        ],
        out_specs=(row_block, row_block, tail_block, tail_block),
        compiler_params=pltpu.CompilerParams(
            dimension_semantics=("arbitrary", "arbitrary"), vmem_limit_bytes=VMEM_LIMIT),
        name="kv_prompt",
    )(x, g, w)


def _kv_sample_kernel(x_ref, g_ref, w_ref, k_ref, v_ref):
    kv = _dot(_rms(x_ref[...], g_ref[...]).astype(BF16), w_ref[...])
    k_ref[...] = kv[:, :D_MODEL]
    v_ref[...] = kv[:, D_MODEL:]


def _kv_sample(x, g, w):
    out = jax.ShapeDtypeStruct(x.shape, F32)
    return pl.pallas_call(
        _kv_sample_kernel,
        out_shape=(out, out),
        compiler_params=pltpu.CompilerParams(vmem_limit_bytes=VMEM_LIMIT),
        name="kv_sample",
    )(x, g, w)


def _softmax_pv(s, v2):
    m = jnp.max(s, axis=-1, keepdims=True)
    p = jnp.exp(s - m)
    denom = jnp.sum(p, axis=-1, keepdims=True)
    return _dot(p.astype(BF16), v2) / denom


def _attn_prompt_kernel(x_ref, g_ref, k_ref, v_ref, wq_ref, wo_ref, bias_ref, o_ref, o_scr, *, q_rows, k_rows):
    j = pl.program_id(1)
    x = x_ref[0]
    xn = _rms(x, g_ref[2:3, :]).astype(BF16)
    q = (_dot(xn, wq_ref[...]) * SCALE).astype(BF16)
    row0 = pl.multiple_of(j * q_rows, q_rows)
    key_row = row0 + lax.broadcasted_iota(jnp.int32, (1, k_rows), 1)
    valid = key_row >= BAND_PAST
    first_head = lax.broadcasted_iota(jnp.int32, (1, LANES), 1) < HEAD_DIM
    zero = jnp.zeros((), BF16)
    for hp in range(HEAD_PAIRS):
        lanes = slice(hp * LANES, (hp + 1) * LANES)
        q2 = q[:, lanes]
        k2 = k_ref[0, pl.ds(row0, k_rows), lanes]
        v2 = v_ref[0, pl.ds(row0, k_rows), lanes]
        outs = []
        for half, qh in enumerate((jnp.where(first_head, q2, zero), jnp.where(first_head, zero, q2))):
            s = _dot_nt(qh, k2) + bias_ref[2 * hp + half]
            s = jnp.where(valid, s, NEG_INF)
            outs.append(_softmax_pv(s, v2))
        o_scr[:, lanes] = jnp.where(first_head, outs[0], outs[1])
    y = _dot(o_scr[...].astype(BF16), wo_ref[...])
    o_ref[0] = x + _rms(y, g_ref[3:4, :])


def _attn_prompt(x, gains, kp, vp, wq, wo, bias):
    n, l, _ = x.shape
    q_rows = ATT_QCHUNKS * CHUNK
    k_rows = (ATT_QCHUNKS + N_LEFT_CHUNKS) * CHUNK
    kv_block = pl.BlockSpec((1, kp.shape[1], D_MODEL), lambda b, j: (b, 0, 0))
    return pl.pallas_call(
        functools.partial(_attn_prompt_kernel, q_rows=q_rows, k_rows=k_rows),
        out_shape=jax.ShapeDtypeStruct(x.shape, F32),
        grid=(n, l // q_rows),
        in_specs=[
            pl.BlockSpec((1, q_rows, D_MODEL), lambda b, j: (b, j, 0)),
            _const_spec(gains.shape),
            kv_block, kv_block,
            _const_spec(wq.shape), _const_spec(wo.shape), _const_spec(bias.shape),
        ],
        out_specs=pl.BlockSpec((1, q_rows, D_MODEL), lambda b, j: (b, j, 0)),
        scratch_shapes=[pltpu.VMEM((q_rows, D_MODEL), F32)],
        compiler_params=pltpu.CompilerParams(
            dimension_semantics=("arbitrary", "arbitrary"), vmem_limit_bytes=VMEM_LIMIT),
        name="attn_prompt",
    )(x, gains, kp, vp, wq, wo, bias)


def _attn_sample_kernel(x_ref, g_ref, ck_ref, cv_ref, nk_ref, nv_ref, wq_ref, wo_ref, bc_ref, bn_ref,
                        o_ref, o_scr):
    x = x_ref[0]
    xn = _rms(x, g_ref[2:3, :]).astype(BF16)
    q = (_dot(xn, wq_ref[...]) * SCALE).astype(BF16)
    first_head = lax.broadcasted_iota(jnp.int32, (1, LANES), 1) < HEAD_DIM
    zero = jnp.zeros((), BF16)
    for hp in range(HEAD_PAIRS):
        lanes = slice(hp * LANES, (hp + 1) * LANES)
        q2 = q[:, lanes]
        ck = ck_ref[0, :, lanes].astype(BF16)
        cv = cv_ref[0, :, lanes].astype(BF16)
        nk = nk_ref[0, :, lanes].astype(BF16)
        nv = nv_ref[0, :, lanes].astype(BF16)
        outs = []
        for half, qh in enumerate((jnp.where(first_head, q2, zero), jnp.where(first_head, zero, q2))):
            h = 2 * hp + half
            sc = _dot_nt(qh, ck) + bc_ref[h]
            sn = _dot_nt(qh, nk) + bn_ref[h]
            m = jnp.maximum(jnp.max(sc, axis=-1, keepdims=True), jnp.max(sn, axis=-1, keepdims=True))
            pc = jnp.exp(sc - m)
            pn = jnp.exp(sn - m)
            denom = jnp.sum(pc, axis=-1, keepdims=True) + jnp.sum(pn, axis=-1, keepdims=True)
            outs.append((_dot(pc.astype(BF16), cv) + _dot(pn.astype(BF16), nv)) / denom)
        o_scr[:, lanes] = jnp.where(first_head, outs[0], outs[1])
    y = _dot(o_scr[...].astype(BF16), wo_ref[...])
    o_ref[0] = x + _rms(y, g_ref[3:4, :])


def _attn_sample(x, gains, ck, cv, nk, nv, wq, wo, bias_c, bias_n):
    n, s, _ = x.shape
    r = ck.shape[1]
    seq_block = pl.BlockSpec((1, s, D_MODEL), lambda b: (b, 0, 0))
    cache_block = pl.BlockSpec((1, r, D_MODEL), lambda b: (b, 0, 0))
    return pl.pallas_call(
        _attn_sample_kernel,
        out_shape=jax.ShapeDtypeStruct(x.shape, F32),
        grid=(n,),
        in_specs=[seq_block, _const_spec(gains.shape), cache_block, cache_block, seq_block, seq_block,
                  _const_spec(wq.shape), _const_spec(wo.shape),
                  _const_spec(bias_c.shape), _const_spec(bias_n.shape)],
        out_specs=seq_block,
        scratch_shapes=[pltpu.VMEM((s, D_MODEL), F32)],
        compiler_params=pltpu.CompilerParams(
            dimension_semantics=("arbitrary",), vmem_limit_bytes=VMEM_LIMIT),
        name="attn_sample",
    )(x, gains, ck, cv, nk, nv, wq, wo, bias_c, bias_n)


def _prompt_bias(table):
    q_rows = ATT_QCHUNKS * CHUNK
    k_rows = (ATT_QCHUNKS + N_LEFT_CHUNKS) * CHUNK
    qi = np.arange(q_rows)[:, None]
    kj = np.arange(k_rows)[None, :]
    rel = np.clip(qi + BAND_PAST - kj, -MAX_REL, MAX_REL) + MAX_REL
    lo = (qi // CHUNK) * CHUNK
    in_band = (kj >= lo) & (kj < lo + BAND)
    bias = table.astype(F32)[:, rel]
    return jnp.where(in_band[None], bias, NEG_INF)


def _sample_bias(table, s, r):
    q_pos = PAST_LEN + np.arange(s)
    k_pos = np.concatenate([PAST_LEN - r + np.arange(r), q_pos])
    rel = np.clip(q_pos[:, None] - k_pos[None, :], -MAX_REL, MAX_REL) + MAX_REL
    qc = q_pos // CHUNK
    kc = k_pos // CHUNK
    valid = (kc[None, :] <= qc[:, None]) & (kc[None, :] >= qc[:, None] - N_LEFT_CHUNKS)
    bias = jnp.where(valid[None], table.astype(F32)[:, rel], NEG_INF)
    return bias[:, :, :r], bias[:, :, r:]


def kernel(x_prompt, x_sample, state_ssm_re, state_ssm_im, cache_k, cache_v, norm_gains, ffn1_w_in, ffn1_w_out, ffn2_w_in, ffn2_w_out, ssm_lambda_re, ssm_lambda_im, ssm_log_step, ssm_b_re, ssm_b_im, ssm_c_re, ssm_c_im, ssm_d, ssm_w_glu, kv_norm, w_kv, attn_w_q, attn_w_o, attn_rel_bias):
    bp, lp, _ = x_prompt.shape
    bs, ls, _ = x_sample.shape
    cache_rows = cache_k.shape[1]
    assert bp == SUBLANES and bs == SUBLANES and lp % BAND_PAST == 0

    gains = norm_gains.astype(F32)
    w1_in, w1_out = ffn1_w_in.astype(BF16), ffn1_w_out.astype(BF16)
    w2_in, w2_out = ffn2_w_in.astype(BF16), ffn2_w_out.astype(BF16)
    kv_g = kv_norm.astype(F32).reshape(1, D_MODEL)
    w_kv_b = w_kv.astype(BF16)
    wq_b, wo_b = attn_w_q.astype(BF16), attn_w_o.astype(BF16)
    s5 = [_s5_params(ssm_lambda_re[a], ssm_lambda_im[a], ssm_log_step[a], ssm_b_re[a], ssm_b_im[a],
                     ssm_c_re[a], ssm_c_im[a], ssm_d[a], ssm_w_glu[a], SUBLANES)
          for a in range(N_A_LAYERS)]

    def s5_layers(x, h0_re, h0_im):
        states_re, states_im = [], []
        for a in range(N_A_LAYERS):
            x = _half_ffn(x, gains[a], w1_in[a], w1_out[a], 0, 1)
            x, hr, hi = _s5_layer(x, gains[a], s5[a], h0_re[a], h0_im[a], SUBLANES)
            x = _half_ffn(x, gains[a], w2_in[a], w2_out[a], 4, 5)
            states_re.append(hr.reshape(SUBLANES, N_GROUPS, SSM_STATE))
            states_im.append(hi.reshape(SUBLANES, N_GROUPS, SSM_STATE))
        return x, jnp.stack(states_re), jnp.stack(states_im)

    zeros = jnp.zeros((N_A_LAYERS, bp, STATE_COLS), F32)
    xt = x_prompt.astype(F32).transpose(1, 0, 2).reshape(lp * bp, D_MODEL)
    xt, ssm_re_p, ssm_im_p = s5_layers(xt, zeros, zeros)
    xb = xt.reshape(lp, bp, D_MODEL).transpose(1, 0, 2)
    kp, vp, k_tail, v_tail = _kv_prompt(xb, kv_g, w_kv_b)
    for b in range(DEPTH - N_A_LAYERS):
        layer = N_A_LAYERS + b
        flat = _half_ffn(xb.reshape(bp * lp, D_MODEL), gains[layer], w1_in[layer], w1_out[layer], 0, 1)
        xb = _attn_prompt(flat.reshape(bp, lp, D_MODEL), gains[layer], kp, vp, wq_b[b], wo_b[b],
                          _prompt_bias(attn_rel_bias[b]))
        flat = _half_ffn(xb.reshape(bp * lp, D_MODEL), gains[layer], w2_in[layer], w2_out[layer], 4, 5)
        xb = flat.reshape(bp, lp, D_MODEL)
    y_prompt = xb
    k_prompt = k_tail.reshape(bp, BAND_PAST, N_HEADS, HEAD_DIM)
    v_prompt = v_tail.reshape(bp, BAND_PAST, N_HEADS, HEAD_DIM)

    h0_re = state_ssm_re.astype(F32).reshape(N_A_LAYERS, bs, STATE_COLS)
    h0_im = state_ssm_im.astype(F32).reshape(N_A_LAYERS, bs, STATE_COLS)
    st = x_sample.astype(F32).transpose(1, 0, 2).reshape(ls * bs, D_MODEL)
    st, ssm_re_s, ssm_im_s = s5_layers(st, h0_re, h0_im)
    sb = st.reshape(ls, bs, D_MODEL).transpose(1, 0, 2).reshape(bs * ls, D_MODEL)
    k_new, v_new = _kv_sample(sb, kv_g, w_kv_b)
    k_new = k_new.reshape(bs, ls, D_MODEL)
    v_new = v_new.reshape(bs, ls, D_MODEL)
    ck = cache_k.astype(F32).reshape(bs, cache_rows, D_MODEL)
    cv = cache_v.astype(F32).reshape(bs, cache_rows, D_MODEL)
    for b in range(DEPTH - N_A_LAYERS):
        layer = N_A_LAYERS + b
        sb = _half_ffn(sb, gains[layer], w1_in[layer], w1_out[layer], 0, 1)
        bias_c, bias_n = _sample_bias(attn_rel_bias[b], ls, cache_rows)
        sb = _attn_sample(sb.reshape(bs, ls, D_MODEL), gains[layer], ck, cv, k_new, v_new,
                          wq_b[b], wo_b[b], bias_c, bias_n).reshape(bs * ls, D_MODEL)
        sb = _half_ffn(sb, gains[layer], w2_in[layer], w2_out[layer], 4, 5)
    y_sample = sb.reshape(bs, ls, D_MODEL)
    k_sample = k_new.reshape(bs, ls, N_HEADS, HEAD_DIM)
    v_sample = v_new.reshape(bs, ls, N_HEADS, HEAD_DIM)

    return (y_prompt, y_sample, ssm_re_p, ssm_im_p, k_prompt, v_prompt,
            ssm_re_s, ssm_im_s, k_sample, v_sample)
```

```python
import functools
import math

import numpy as np
import jax
import jax.numpy as jnp
from jax import lax
from jax.experimental import pallas as pl
from jax.experimental.pallas import tpu as pltpu

D_MODEL = 1024
DEPTH = 4
PAST_LEN = 4096
CHUNK = 64
N_A_LAYERS = DEPTH // 2
D_FF = 2816
SSM_GROUP = 16
N_GROUPS = D_MODEL // SSM_GROUP
SSM_STATE = 64
N_HEADS = 16
HEAD_DIM = D_MODEL // N_HEADS
N_LEFT_CHUNKS = 8
BAND_PAST = N_LEFT_CHUNKS * CHUNK
BAND = (N_LEFT_CHUNKS + 1) * CHUNK
MAX_REL = 128
EPS = 1e-6
SCALE = HEAD_DIM ** -0.5
NEG_INF = -1e30

F32 = jnp.float32
BF16 = jnp.bfloat16

LANES = 128
SUBLANES = 8
VMEM_LIMIT = 56 * 1024 * 1024

FFN_ROWS = 512
FFN_CHUNK = 256
S5_STEPS = 64
S5_COLS = 1024
GROUPS_PER_BLOCK = LANES // SSM_GROUP
N_CBLOCKS = D_MODEL // LANES
STATE_COLS = N_GROUPS * SSM_STATE
BLOCK_STATE = GROUPS_PER_BLOCK * SSM_STATE
ATT_QROWS = 256
KEY_BLOCK = 128
HEAD_PAIRS = N_HEADS // 2
LOG2E = 1.4426950408889634


def _const_spec(shape):
    zeros = (0,) * len(shape)
    return pl.BlockSpec(shape, lambda *_: zeros, pipeline_mode=pl.Buffered(1))


def _rms(x, g):
    return x * lax.rsqrt(jnp.mean(x * x, axis=-1, keepdims=True) + EPS) * g


def _sigmoid(x):
    return 1.0 / (1.0 + jnp.exp(-x))


def _gelu_tanh(x):
    c = math.sqrt(2.0 / math.pi)
    return 0.5 * x * (1.0 + jnp.tanh(c * (x + 0.044715 * (x * x * x))))


def _dot(a, b):
    return jnp.dot(a, b, preferred_element_type=F32)


def _dot_nt(a, b):
    return lax.dot_general(a, b, (((1,), (1,)), ((), ())), preferred_element_type=F32)


def _ffn_kernel(x_ref, g_ref, w_in_ref, w_out_ref, o_ref, *, pre, post):
    x = x_ref[...]
    xn = _rms(x, g_ref[pre:pre + 1, :]).astype(BF16)
    acc = jnp.zeros(x.shape, F32)
    for c in range(D_FF // FFN_CHUNK):
        lo = c * FFN_CHUNK
        gate = _dot(xn, w_in_ref[:, lo:lo + FFN_CHUNK])
        up = _dot(xn, w_in_ref[:, D_FF + lo:D_FF + lo + FFN_CHUNK])
        act = (gate * _sigmoid(gate) * up).astype(BF16)
        acc = acc + _dot(act, w_out_ref[lo:lo + FFN_CHUNK, :])
    o_ref[...] = x + 0.5 * _rms(acc, g_ref[post:post + 1, :])


def _half_ffn(x, gains, w_in, w_out, pre, post):
    rows = x.shape[0]
    tm = min(FFN_ROWS, rows)
    return pl.pallas_call(
        functools.partial(_ffn_kernel, pre=pre, post=post),
        out_shape=jax.ShapeDtypeStruct(x.shape, F32),
        grid=(rows // tm,),
        in_specs=[
            pl.BlockSpec((tm, D_MODEL), lambda i: (i, 0)),
            _const_spec(gains.shape),
            _const_spec(w_in.shape),
            _const_spec(w_out.shape),
        ],
        out_specs=pl.BlockSpec((tm, D_MODEL), lambda i: (i, 0)),
        compiler_params=pltpu.CompilerParams(
            dimension_semantics=("arbitrary",), vmem_limit_bytes=VMEM_LIMIT),
        name="half_ffn",
    )(x, gains, w_in, w_out)


def _s5_kernel(x_ref, g_ref, b_ref, cre_ref, cim_ref, are_ref, aim_ref,
               h0re_ref, h0im_ref, d_ref, wglu_ref,
               o_ref, hre_out, him_out,
               bure, buim, hre, him, y_scr, *, steps, batch):
    i = pl.program_id(0)

    @pl.when(i == 0)
    def _():
        hre[...] = h0re_ref[...]
        him[...] = h0im_ref[...]

    x = x_ref[...]
    xn = _rms(x, g_ref[2:3, :])
    xb = xn.astype(BF16)

    for k in range(N_CBLOCKS):
        bu = _dot(xb[:, k * LANES:(k + 1) * LANES], b_ref[k])
        bure[:, k * BLOCK_STATE:(k + 1) * BLOCK_STATE] = bu[:, :BLOCK_STATE]
        buim[:, k * BLOCK_STATE:(k + 1) * BLOCK_STATE] = bu[:, BLOCK_STATE:]

    for cb in range(STATE_COLS // S5_COLS):
        cols = slice(cb * S5_COLS, (cb + 1) * S5_COLS)
        ar = are_ref[:, cols]
        ai = aim_ref[:, cols]

        def step(t, carry, cols=cols, ar=ar, ai=ai):
            hr, hi = carry
            rows = pl.ds(pl.multiple_of(t * batch, batch), batch)
            nhr = ar * hr - ai * hi + bure[rows, cols]
            nhi = ar * hi + ai * hr + buim[rows, cols]
            bure[rows, cols] = nhr
            buim[rows, cols] = nhi
            return nhr, nhi

        hr, hi = lax.fori_loop(0, steps, step, (hre[:, cols], him[:, cols]), unroll=2)
        hre[:, cols] = hr
        him[:, cols] = hi

    hre_out[...] = hre[...]
    him_out[...] = him[...]

    for k in range(N_CBLOCKS):
        sc = slice(k * BLOCK_STATE, (k + 1) * BLOCK_STATE)
        yk = _dot(bure[:, sc].astype(BF16), cre_ref[k]) + _dot(buim[:, sc].astype(BF16), cim_ref[k])
        y_scr[:, k * LANES:(k + 1) * LANES] = yk
    y = _gelu_tanh(y_scr[...] + d_ref[...] * xn)
    z = _dot(y.astype(BF16), wglu_ref[...])
    out = z[:, :D_MODEL] * _sigmoid(z[:, D_MODEL:])
    o_ref[...] = x + _rms(out, g_ref[3:4, :])


def _s5_layer(x, gains, p, h0re, h0im, batch):
    rows = x.shape[0]
    steps = min(S5_STEPS, rows // batch)
    tr = steps * batch
    consts = [gains, p["b"], p["cre"], p["cim"], p["are"], p["aim"],
              h0re, h0im, p["d"], p["wglu"]]
    state = jax.ShapeDtypeStruct((batch, STATE_COLS), F32)
    return pl.pallas_call(
        functools.partial(_s5_kernel, steps=steps, batch=batch),
        out_shape=(jax.ShapeDtypeStruct(x.shape, F32), state, state),
        grid=(rows // tr,),
        in_specs=[pl.BlockSpec((tr, D_MODEL), lambda i: (i, 0))] + [_const_spec(c.shape) for c in consts],
        out_specs=(pl.BlockSpec((tr, D_MODEL), lambda i: (i, 0)),
                   pl.BlockSpec((batch, STATE_COLS), lambda i: (0, 0)),
                   pl.BlockSpec((batch, STATE_COLS), lambda i: (0, 0))),
        scratch_shapes=[
            pltpu.VMEM((tr, STATE_COLS), F32),
            pltpu.VMEM((tr, STATE_COLS), F32),
            pltpu.VMEM((batch, STATE_COLS), F32),
            pltpu.VMEM((batch, STATE_COLS), F32),
            pltpu.VMEM((tr, D_MODEL), F32),
        ],
        compiler_params=pltpu.CompilerParams(
            dimension_semantics=("arbitrary",), vmem_limit_bytes=VMEM_LIMIT),
        name="s5_mixer",
    )(x, *consts)


def _s5_params(lam_re, lam_im, log_step, b_re, b_im, c_re, c_im, d_skip, w_glu, batch):
    step = jnp.exp(log_step.astype(F32))[:, None]
    lr = lam_re.astype(F32)
    li = lam_im.astype(F32)
    mag = jnp.exp(lr * step)
    a_re = mag * jnp.cos(li * step)
    a_im = mag * jnp.sin(li * step)
    den = lr * lr + li * li
    coef_re = ((a_re - 1.0) * lr + a_im * li) / den
    coef_im = (a_im * lr - (a_re - 1.0) * li) / den
    bre = b_re.astype(F32)
    bim = b_im.astype(F32)
    bb_re = coef_re[..., None] * bre - coef_im[..., None] * bim
    bb_im = coef_re[..., None] * bim + coef_im[..., None] * bre

    eye = jnp.eye(GROUPS_PER_BLOCK, dtype=F32)

    def pack_b(b):
        b4 = b.reshape(N_CBLOCKS, GROUPS_PER_BLOCK, SSM_STATE, SSM_GROUP).transpose(0, 1, 3, 2)
        full = b4[:, :, :, None, :] * eye[None, :, None, :, None]
        return full.reshape(N_CBLOCKS, LANES, BLOCK_STATE)

    def pack_c(c):
        c4 = c.reshape(N_CBLOCKS, GROUPS_PER_BLOCK, SSM_GROUP, SSM_STATE).transpose(0, 1, 3, 2)
        full = c4[:, :, :, None, :] * eye[None, :, None, :, None]
        return full.reshape(N_CBLOCKS, BLOCK_STATE, LANES)

    return {
        "b": jnp.concatenate([pack_b(bb_re), pack_b(bb_im)], axis=-1).astype(BF16),
        "cre": pack_c(c_re.astype(F32)).astype(BF16),
        "cim": pack_c(-c_im.astype(F32)).astype(BF16),
        "are": jnp.broadcast_to(a_re.reshape(1, STATE_COLS), (batch, STATE_COLS)),
        "aim": jnp.broadcast_to(a_im.reshape(1, STATE_COLS), (batch, STATE_COLS)),
        "d": d_skip.astype(F32).reshape(1, D_MODEL),
        "wglu": w_glu.astype(BF16),
    }


def _kv_prompt_kernel(x_ref, g_ref, w_ref, kp_ref, vtp_ref, k_ref, v_ref):
    i = pl.program_id(1)

    @pl.when(i == 0)
    def _():
        kp_ref[...] = jnp.zeros(kp_ref.shape, BF16)
        vtp_ref[...] = jnp.zeros(vtp_ref.shape, BF16)

    @pl.when(i > 0)
    def _():
        kv = _dot(_rms(x_ref[0], g_ref[...]).astype(BF16), w_ref[...])
        kp_ref[0] = kv[:, :D_MODEL].astype(BF16)
        vtp_ref[0] = kv[:, D_MODEL:].T.astype(BF16)

        @pl.when(i == pl.num_programs(1) - 1)
        def _():
            k_ref[0] = kv[:, :D_MODEL]
            v_ref[0] = kv[:, D_MODEL:]


def _kv_prompt(x, g, w):
    n, l, _ = x.shape
    tiles = l // BAND_PAST
    padded = jax.ShapeDtypeStruct((n, BAND_PAST + l, D_MODEL), BF16)
    padded_t = jax.ShapeDtypeStruct((n, D_MODEL, BAND_PAST + l), BF16)
    tail = jax.ShapeDtypeStruct((n, BAND_PAST, D_MODEL), F32)
    row_block = pl.BlockSpec((1, BAND_PAST, D_MODEL), lambda b, i: (b, i, 0))
    col_block = pl.BlockSpec((1, D_MODEL, BAND_PAST), lambda b, i: (b, 0, i))
    tail_block = pl.BlockSpec((1, BAND_PAST, D_MODEL), lambda b, i: (b, 0, 0))
    return pl.pallas_call(
        _kv_prompt_kernel,
        out_shape=(padded, padded_t, tail, tail),
        grid=(n, tiles + 1),
        in_specs=[
            pl.BlockSpec((1, BAND_PAST, D_MODEL), lambda b, i: (b, jnp.maximum(i - 1, 0), 0)),
            _const_spec(g.shape),
            _const_spec(w.shape),
        ],
        out_specs=(row_block, col_block, tail_block, tail_block),
        compiler_params=pltpu.CompilerParams(
            dimension_semantics=("arbitrary", "arbitrary"), vmem_limit_bytes=VMEM_LIMIT),
        name="kv_prompt",
    )(x, g, w)


def _kv_sample_kernel(x_ref, g_ref, w_ref, k_ref, v_ref):
    kv = _dot(_rms(x_ref[...], g_ref[...]).astype(BF16), w_ref[...])
    k_ref[...] = kv[:, :D_MODEL]
    v_ref[...] = kv[:, D_MODEL:]


def _kv_sample(x, g, w):
    out = jax.ShapeDtypeStruct(x.shape, F32)
    return pl.pallas_call(
        _kv_sample_kernel,
        out_shape=(out, out),
        compiler_params=pltpu.CompilerParams(vmem_limit_bytes=VMEM_LIMIT),
        name="kv_sample",
    )(x, g, w)


def _attn_prompt_kernel(x_ref, g_ref, k_ref, vt_ref, wq_ref, wo_ref, base_ref, o_ref,
                        bias_scr, ot_scr, q_scr, s_scr, p_scr, *, q_rows, k_rows):
    b = pl.program_id(0)
    j = pl.program_id(1)
    width = base_ref.shape[1]
    n_kb = k_rows // KEY_BLOCK
    n_pairs = HEAD_PAIRS

    @pl.when((b == 0) & (j == 0))
    def _():
        kj = lax.broadcasted_iota(jnp.int32, (k_rows, q_rows), 0)
        qi = lax.broadcasted_iota(jnp.int32, (k_rows, q_rows), 1)
        lo = (qi // CHUNK) * CHUNK
        in_band = (kj >= lo) & (kj < lo + BAND)
        for h in range(N_HEADS):
            rows = jnp.broadcast_to(base_ref[h:h + 1, :], (k_rows, width))
            toeplitz = pltpu.roll(rows, 0, 1, stride=1, stride_axis=0)
            bias_scr[h] = jnp.where(in_band, toeplitz[:, :q_rows] * LOG2E, NEG_INF)

    x = x_ref[0]
    xn = _rms(x, g_ref[2:3, :]).astype(BF16)
    q_scr[...] = (_dot(xn, wq_ref[...]) * (SCALE * LOG2E)).astype(BF16)
    row0 = pl.multiple_of(j * q_rows, q_rows)
    lane = lax.broadcasted_iota(jnp.int32, (1, LANES), 1)
    zero = jnp.zeros((), BF16)

    def scores(hp, masked):
        lanes = pl.ds(pl.multiple_of(hp * LANES, LANES), LANES)
        q2 = q_scr[:, lanes]
        maxima = []
        for half in range(2):
            h = 2 * hp + half
            qh = jnp.where((lane < HEAD_DIM) == (half == 0), q2, zero)
            k2 = k_ref[0, pl.ds(row0, k_rows), lanes]
            st = _dot_nt(k2, qh) + bias_scr[h]
            if masked:
                key_row = row0 + lax.broadcasted_iota(jnp.int32, st.shape, 0)
                st = jnp.where(key_row >= BAND_PAST, st, NEG_INF)
            s_scr[half] = st
            maxima.append(jnp.max(st, axis=0, keepdims=True))
        return tuple(maxima)

    def probs(maxima):
        for half in range(2):
            for kb in range(n_kb):
                rows = pl.ds(kb * KEY_BLOCK, KEY_BLOCK)
                p_scr[half, rows, :] = jnp.exp2(s_scr[half, rows, :] - maxima[half]).astype(BF16)

    ones_rows = jnp.ones((2 * SUBLANES, k_rows), BF16)

    def weighted_values(hp):
        for half in range(2):
            h = 2 * hp + half
            rows = pl.ds(pl.multiple_of(h * HEAD_DIM, HEAD_DIM), HEAD_DIM)
            vt = jnp.concatenate([vt_ref[0, rows, pl.ds(row0, k_rows)], ones_rows], axis=0)
            ot = _dot(vt, p_scr[half])
            ot_scr[rows, :] = ot[:HEAD_DIM] * (1.0 / ot[HEAD_DIM:HEAD_DIM + 1])

    def heads(masked):
        m = scores(0, masked)
        probs(m)
        m = scores(1, masked)

        def body(t, m_prev):
            weighted_values(t - 2)
            probs(m_prev)
            return scores(t, masked)

        m = lax.fori_loop(2, n_pairs, body, m)
        weighted_values(n_pairs - 2)
        probs(m)
        weighted_values(n_pairs - 1)

    needs_mask = row0 < BAND_PAST

    @pl.when(needs_mask)
    def _():
        heads(True)

    @pl.when(jnp.logical_not(needs_mask))
    def _():
        heads(False)

    y = _dot(ot_scr[...].T.astype(BF16), wo_ref[...])
    o_ref[0] = x + _rms(y, g_ref[3:4, :])


def _bias_base(table, q_rows, k_rows):
    width = -(-(q_rows + k_rows) // LANES) * LANES
    d = np.zeros((width,), np.int64)
    d[:q_rows] = np.arange(q_rows)
    d[width - (k_rows - 1):] = -np.arange(k_rows - 1, 0, -1)
    idx = np.clip(d + BAND_PAST, -MAX_REL, MAX_REL) + MAX_REL
    return jnp.take(table.astype(F32), jnp.asarray(idx, jnp.int32), axis=1)


def _attn_prompt(x, gains, kp, vtp, wq, wo, table):
    n, l, _ = x.shape
    q_rows = min(ATT_QROWS, l)
    k_rows = q_rows + BAND_PAST
    base = _bias_base(table, q_rows, k_rows)
    return pl.pallas_call(
        functools.partial(_attn_prompt_kernel, q_rows=q_rows, k_rows=k_rows),
        out_shape=jax.ShapeDtypeStruct(x.shape, F32),
        grid=(n, l // q_rows),
        in_specs=[
            pl.BlockSpec((1, q_rows, D_MODEL), lambda b, j: (b, j, 0)),
            _const_spec(gains.shape),
            pl.BlockSpec((1, kp.shape[1], D_MODEL), lambda b, j: (b, 0, 0)),
            pl.BlockSpec((1, D_MODEL, vtp.shape[2]), lambda b, j: (b, 0, 0)),
            _const_spec(wq.shape), _const_spec(wo.shape), _const_spec(base.shape),
        ],
        out_specs=pl.BlockSpec((1, q_rows, D_MODEL), lambda b, j: (b, j, 0)),
        scratch_shapes=[
            pltpu.VMEM((N_HEADS, k_rows, q_rows), F32),
            pltpu.VMEM((D_MODEL, q_rows), F32),
            pltpu.VMEM((q_rows, D_MODEL), BF16),
            pltpu.VMEM((2, k_rows, q_rows), F32),
            pltpu.VMEM((2, k_rows, q_rows), BF16),
        ],
        compiler_params=pltpu.CompilerParams(
            dimension_semantics=("arbitrary", "arbitrary"), vmem_limit_bytes=VMEM_LIMIT),
        name="attn_prompt",
    )(x, gains, kp, vtp, wq, wo, base)


def _attn_sample_kernel(x_ref, g_ref, ck_ref, cv_ref, nk_ref, nv_ref, wq_ref, wo_ref, bc_ref, bn_ref,
                        o_ref, o_scr):
    x = x_ref[0]
    xn = _rms(x, g_ref[2:3, :]).astype(BF16)
    q = (_dot(xn, wq_ref[...]) * SCALE).astype(BF16)
    first_head = lax.broadcasted_iota(jnp.int32, (1, LANES), 1) < HEAD_DIM
    zero = jnp.zeros((), BF16)
    for hp in range(HEAD_PAIRS):
        lanes = slice(hp * LANES, (hp + 1) * LANES)
        q2 = q[:, lanes]
        ck = ck_ref[0, :, lanes].astype(BF16)
        cv = cv_ref[0, :, lanes].astype(BF16)
        nk = nk_ref[0, :, lanes].astype(BF16)
        nv = nv_ref[0, :, lanes].astype(BF16)
        outs = []
        for half, qh in enumerate((jnp.where(first_head, q2, zero), jnp.where(first_head, zero, q2))):
            h = 2 * hp + half
            sc = _dot_nt(qh, ck) + bc_ref[h]
            sn = _dot_nt(qh, nk) + bn_ref[h]
            m = jnp.maximum(jnp.max(sc, axis=-1, keepdims=True), jnp.max(sn, axis=-1, keepdims=True))
            pc = jnp.exp(sc - m)
            pn = jnp.exp(sn - m)
            denom = jnp.sum(pc, axis=-1, keepdims=True) + jnp.sum(pn, axis=-1, keepdims=True)
            outs.append((_dot(pc.astype(BF16), cv) + _dot(pn.astype(BF16), nv)) / denom)
        o_scr[:, lanes] = jnp.where(first_head, outs[0], outs[1])
    y = _dot(o_scr[...].astype(BF16), wo_ref[...])
    o_ref[0] = x + _rms(y, g_ref[3:4, :])


def _attn_sample(x, gains, ck, cv, nk, nv, wq, wo, bias_c, bias_n):
    n, s, _ = x.shape
    r = ck.shape[1]
    seq_block = pl.BlockSpec((1, s, D_MODEL), lambda b: (b, 0, 0))
    cache_block = pl.BlockSpec((1, r, D_MODEL), lambda b: (b, 0, 0))
    return pl.pallas_call(
        _attn_sample_kernel,
        out_shape=jax.ShapeDtypeStruct(x.shape, F32),
        grid=(n,),
        in_specs=[seq_block, _const_spec(gains.shape), cache_block, cache_block, seq_block, seq_block,
                  _const_spec(wq.shape), _const_spec(wo.shape),
                  _const_spec(bias_c.shape), _const_spec(bias_n.shape)],
        out_specs=seq_block,
        scratch_shapes=[pltpu.VMEM((s, D_MODEL), F32)],
        compiler_params=pltpu.CompilerParams(
            dimension_semantics=("arbitrary",), vmem_limit_bytes=VMEM_LIMIT),
        name="attn_sample",
    )(x, gains, ck, cv, nk, nv, wq, wo, bias_c, bias_n)


def _sample_bias(table, s, r):
    q_pos = PAST_LEN + np.arange(s)
    k_pos = np.concatenate([PAST_LEN - r + np.arange(r), q_pos])
    rel = np.clip(q_pos[:, None] - k_pos[None, :], -MAX_REL, MAX_REL) + MAX_REL
    qc = q_pos // CHUNK
    kc = k_pos // CHUNK
    valid = (kc[None, :] <= qc[:, None]) & (kc[None, :] >= qc[:, None] - N_LEFT_CHUNKS)
    bias = jnp.where(valid[None], table.astype(F32)[:, rel], NEG_INF)
    return bias[:, :, :r], bias[:, :, r:]


def kernel(x_prompt, x_sample, state_ssm_re, state_ssm_im, cache_k, cache_v, norm_gains, ffn1_w_in, ffn1_w_out, ffn2_w_in, ffn2_w_out, ssm_lambda_re, ssm_lambda_im, ssm_log_step, ssm_b_re, ssm_b_im, ssm_c_re, ssm_c_im, ssm_d, ssm_w_glu, kv_norm, w_kv, attn_w_q, attn_w_o, attn_rel_bias):
    bp, lp, _ = x_prompt.shape
    bs, ls, _ = x_sample.shape
    cache_rows = cache_k.shape[1]
    assert bp == SUBLANES and bs == SUBLANES and lp % BAND_PAST == 0

    gains = norm_gains.astype(F32)
    kv_g = kv_norm.astype(F32).reshape(1, D_MODEL)
    w_kv_b = w_kv.astype(BF16)
    w1_in = [ffn1_w_in[i].astype(BF16) for i in range(DEPTH)]
    w1_out = [ffn1_w_out[i].astype(BF16) for i in range(DEPTH)]
    w2_in = [ffn2_w_in[i].astype(BF16) for i in range(DEPTH)]
    w2_out = [ffn2_w_out[i].astype(BF16) for i in range(DEPTH)]
    wq_b = [attn_w_q[i].astype(BF16) for i in range(DEPTH - N_A_LAYERS)]
    wo_b = [attn_w_o[i].astype(BF16) for i in range(DEPTH - N_A_LAYERS)]
    s5 = [_s5_params(ssm_lambda_re[a], ssm_lambda_im[a], ssm_log_step[a], ssm_b_re[a], ssm_b_im[a],
                     ssm_c_re[a], ssm_c_im[a], ssm_d[a], ssm_w_glu[a], SUBLANES)
          for a in range(N_A_LAYERS)]

    def s5_layers(x, h0_re, h0_im):
        states_re, states_im = [], []
        for a in range(N_A_LAYERS):
            x = _half_ffn(x, gains[a], w1_in[a], w1_out[a], 0, 1)
            x, hr, hi = _s5_layer(x, gains[a], s5[a], h0_re[a], h0_im[a], SUBLANES)
            x = _half_ffn(x, gains[a], w2_in[a], w2_out[a], 4, 5)
            states_re.append(hr.reshape(SUBLANES, N_GROUPS, SSM_STATE))
            states_im.append(hi.reshape(SUBLANES, N_GROUPS, SSM_STATE))
        return x, jnp.stack(states_re), jnp.stack(states_im)

    zeros = jnp.zeros((N_A_LAYERS, bp, STATE_COLS), F32)
    xt = x_prompt.astype(F32).transpose(1, 0, 2).reshape(lp * bp, D_MODEL)
    xt, ssm_re_p, ssm_im_p = s5_layers(xt, zeros, zeros)
    xb = xt.reshape(lp, bp, D_MODEL).transpose(1, 0, 2)
    kp, vtp, k_tail, v_tail = _kv_prompt(xb, kv_g, w_kv_b)
    for b in range(DEPTH - N_A_LAYERS):
        layer = N_A_LAYERS + b
        flat = _half_ffn(xb.reshape(bp * lp, D_MODEL), gains[layer], w1_in[layer], w1_out[layer], 0, 1)
        xb = _attn_prompt(flat.reshape(bp, lp, D_MODEL), gains[layer], kp, vtp, wq_b[b], wo_b[b],
                          attn_rel_bias[b])
        flat = _half_ffn(xb.reshape(bp * lp, D_MODEL), gains[layer], w2_in[layer], w2_out[layer], 4, 5)
        xb = flat.reshape(bp, lp, D_MODEL)
    y_prompt = xb
    k_prompt = k_tail.reshape(bp, BAND_PAST, N_HEADS, HEAD_DIM)
    v_prompt = v_tail.reshape(bp, BAND_PAST, N_HEADS, HEAD_DIM)

    h0_re = state_ssm_re.astype(F32).reshape(N_A_LAYERS, bs, STATE_COLS)
    h0_im = state_ssm_im.astype(F32).reshape(N_A_LAYERS, bs, STATE_COLS)
    st = x_sample.astype(F32).transpose(1, 0, 2).reshape(ls * bs, D_MODEL)
    st, ssm_re_s, ssm_im_s = s5_layers(st, h0_re, h0_im)
    sb = st.reshape(ls, bs, D_MODEL).transpose(1, 0, 2).reshape(bs * ls, D_MODEL)
    k_new, v_new = _kv_sample(sb, kv_g, w_kv_b)
    k_new = k_new.reshape(bs, ls, D_MODEL)
    v_new = v_new.reshape(bs, ls, D_MODEL)
    ck = cache_k.astype(F32).reshape(bs, cache_rows, D_MODEL)
    cv = cache_v.astype(F32).reshape(bs, cache_rows, D_MODEL)
    for b in range(DEPTH - N_A_LAYERS):
        layer = N_A_LAYERS + b
        sb = _half_ffn(sb, gains[layer], w1_in[layer], w1_out[layer], 0, 1)
        bias_c, bias_n = _sample_bias(attn_rel_bias[b], ls, cache_rows)
        sb = _attn_sample(sb.reshape(bs, ls, D_MODEL), gains[layer], ck, cv, k_new, v_new,
                          wq_b[b], wo_b[b], bias_c, bias_n).reshape(bs * ls, D_MODEL)
        sb = _half_ffn(sb, gains[layer], w2_in[layer], w2_out[layer], 4, 5)
    y_sample = sb.reshape(bs, ls, D_MODEL)
    k_sample = k_new.reshape(bs, ls, N_HEADS, HEAD_DIM)
    v_sample = v_new.reshape(bs, ls, N_HEADS, HEAD_DIM)

    return (y_prompt, y_sample, ssm_re_p, ssm_im_p, k_prompt, v_prompt,
            ssm_re_s, ssm_im_s, k_sample, v_sample)
```

```python
import functools
import math

import numpy as np
import jax
import jax.numpy as jnp
from jax import lax
from jax.experimental import pallas as pl
from jax.experimental.pallas import tpu as pltpu

D_MODEL = 1024
DEPTH = 4
PAST_LEN = 4096
CHUNK = 64
N_A_LAYERS = DEPTH // 2
D_FF = 2816
SSM_GROUP = 16
N_GROUPS = D_MODEL // SSM_GROUP
SSM_STATE = 64
N_HEADS = 16
HEAD_DIM = D_MODEL // N_HEADS
N_LEFT_CHUNKS = 8
BAND_PAST = N_LEFT_CHUNKS * CHUNK
BAND = (N_LEFT_CHUNKS + 1) * CHUNK
MAX_REL = 128
EPS = 1e-6
SCALE = HEAD_DIM ** -0.5
NEG_INF = -1e30

F32 = jnp.float32
BF16 = jnp.bfloat16

LANES = 128
SUBLANES = 8
VMEM_LIMIT = 56 * 1024 * 1024

FFN_ROWS = 512
FFN_CHUNK = 256
S5_STEPS = 64
S5_COLS = 1024
GROUPS_PER_BLOCK = LANES // SSM_GROUP
N_CBLOCKS = D_MODEL // LANES
STATE_COLS = N_GROUPS * SSM_STATE
BLOCK_STATE = GROUPS_PER_BLOCK * SSM_STATE
ATT_QROWS = 256
KEY_BLOCK = 128
HEAD_PAIRS = N_HEADS // 2
LOG2E = 1.4426950408889634


def _const_spec(shape):
    zeros = (0,) * len(shape)
    return pl.BlockSpec(shape, lambda *_: zeros, pipeline_mode=pl.Buffered(1))


def _rms(x, g):
    return x * lax.rsqrt(jnp.mean(x * x, axis=-1, keepdims=True) + EPS) * g


def _sigmoid(x):
    return 1.0 / (1.0 + jnp.exp(-x))


def _gelu_tanh(x):
    c = math.sqrt(2.0 / math.pi)
    return 0.5 * x * (1.0 + jnp.tanh(c * (x + 0.044715 * (x * x * x))))


def _dot(a, b):
    return jnp.dot(a, b, preferred_element_type=F32)


def _dot_nt(a, b):
    return lax.dot_general(a, b, (((1,), (1,)), ((), ())), preferred_element_type=F32)


def _ffn_kernel(x_ref, g_ref, w_in_ref, w_out_ref, o_ref, *, pre, post):
    x = x_ref[...]
    xn = _rms(x, g_ref[pre:pre + 1, :]).astype(BF16)
    acc = jnp.zeros(x.shape, F32)
    for c in range(D_FF // FFN_CHUNK):
        lo = c * FFN_CHUNK
        gate = _dot(xn, w_in_ref[:, lo:lo + FFN_CHUNK].astype(BF16))
        up = _dot(xn, w_in_ref[:, D_FF + lo:D_FF + lo + FFN_CHUNK].astype(BF16))
        act = (gate * _sigmoid(gate) * up).astype(BF16)
        acc = acc + _dot(act, w_out_ref[lo:lo + FFN_CHUNK, :].astype(BF16))
    o_ref[...] = x + 0.5 * _rms(acc, g_ref[post:post + 1, :])


def _layer_spec(w, layer):
    return pl.BlockSpec((None,) + w.shape[1:], lambda *_: (layer,) + (0,) * (w.ndim - 1),
                        pipeline_mode=pl.Buffered(1))


def _half_ffn(x, gains, w_in, w_out, layer, pre, post):
    rows = x.shape[0]
    tm = min(FFN_ROWS, rows)
    return pl.pallas_call(
        functools.partial(_ffn_kernel, pre=pre, post=post),
        out_shape=jax.ShapeDtypeStruct(x.shape, F32),
        grid=(rows // tm,),
        in_specs=[
            pl.BlockSpec((tm, D_MODEL), lambda i: (i, 0)),
            _layer_spec(gains, layer),
            _layer_spec(w_in, layer),
            _layer_spec(w_out, layer),
        ],
        out_specs=pl.BlockSpec((tm, D_MODEL), lambda i: (i, 0)),
        compiler_params=pltpu.CompilerParams(
            dimension_semantics=("arbitrary",), vmem_limit_bytes=VMEM_LIMIT),
        name="half_ffn",
    )(x, gains, w_in, w_out)


def _s5_kernel(x_ref, g_ref, b_ref, cre_ref, cim_ref, are_ref, aim_ref,
               h0re_ref, h0im_ref, d_ref, wglu_ref,
               o_ref, hre_out, him_out,
               bure, buim, hre, him, y_scr, *, steps, batch):
    i = pl.program_id(0)

    @pl.when(i == 0)
    def _():
        hre[...] = h0re_ref[...]
        him[...] = h0im_ref[...]

    x = x_ref[...]
    xn = _rms(x, g_ref[2:3, :])
    xb = xn.astype(BF16)

    for k in range(N_CBLOCKS):
        bu = _dot(xb[:, k * LANES:(k + 1) * LANES], b_ref[k])
        bure[:, k * BLOCK_STATE:(k + 1) * BLOCK_STATE] = bu[:, :BLOCK_STATE]
        buim[:, k * BLOCK_STATE:(k + 1) * BLOCK_STATE] = bu[:, BLOCK_STATE:]

    for cb in range(STATE_COLS // S5_COLS):
        cols = slice(cb * S5_COLS, (cb + 1) * S5_COLS)
        ar = are_ref[:, cols]
        ai = aim_ref[:, cols]

        def step(t, carry, cols=cols, ar=ar, ai=ai):
            hr, hi = carry
            rows = pl.ds(pl.multiple_of(t * batch, batch), batch)
            nhr = ar * hr - ai * hi + bure[rows, cols]
            nhi = ar * hi + ai * hr + buim[rows, cols]
            bure[rows, cols] = nhr
            buim[rows, cols] = nhi
            return nhr, nhi

        hr, hi = lax.fori_loop(0, steps, step, (hre[:, cols], him[:, cols]), unroll=2)
        hre[:, cols] = hr
        him[:, cols] = hi

    hre_out[...] = hre[...]
    him_out[...] = him[...]

    for k in range(N_CBLOCKS):
        sc = slice(k * BLOCK_STATE, (k + 1) * BLOCK_STATE)
        yk = _dot(bure[:, sc].astype(BF16), cre_ref[k]) + _dot(buim[:, sc].astype(BF16), cim_ref[k])
        y_scr[:, k * LANES:(k + 1) * LANES] = yk
    y = _gelu_tanh(y_scr[...] + d_ref[...] * xn)
    z = _dot(y.astype(BF16), wglu_ref[...])
    out = z[:, :D_MODEL] * _sigmoid(z[:, D_MODEL:])
    o_ref[...] = x + _rms(out, g_ref[3:4, :])


def _s5_layer(x, gains, p, h0re, h0im, batch):
    rows = x.shape[0]
    steps = min(S5_STEPS, rows // batch)
    tr = steps * batch
    consts = [gains, p["b"], p["cre"], p["cim"], p["are"], p["aim"],
              h0re, h0im, p["d"], p["wglu"]]
    state = jax.ShapeDtypeStruct((batch, STATE_COLS), F32)
    return pl.pallas_call(
        functools.partial(_s5_kernel, steps=steps, batch=batch),
        out_shape=(jax.ShapeDtypeStruct(x.shape, F32), state, state),
        grid=(rows // tr,),
        in_specs=[pl.BlockSpec((tr, D_MODEL), lambda i: (i, 0))] + [_const_spec(c.shape) for c in consts],
        out_specs=(pl.BlockSpec((tr, D_MODEL), lambda i: (i, 0)),
                   pl.BlockSpec((batch, STATE_COLS), lambda i: (0, 0)),
                   pl.BlockSpec((batch, STATE_COLS), lambda i: (0, 0))),
        scratch_shapes=[
            pltpu.VMEM((tr, STATE_COLS), F32),
            pltpu.VMEM((tr, STATE_COLS), F32),
            pltpu.VMEM((batch, STATE_COLS), F32),
            pltpu.VMEM((batch, STATE_COLS), F32),
            pltpu.VMEM((tr, D_MODEL), F32),
        ],
        compiler_params=pltpu.CompilerParams(
            dimension_semantics=("arbitrary",), vmem_limit_bytes=VMEM_LIMIT),
        name="s5_mixer",
    )(x, *consts)


def _s5_params(lam_re, lam_im, log_step, b_re, b_im, c_re, c_im, d_skip, w_glu, batch):
    step = jnp.exp(log_step.astype(F32))[:, None]
    lr = lam_re.astype(F32)
    li = lam_im.astype(F32)
    mag = jnp.exp(lr * step)
    a_re = mag * jnp.cos(li * step)
    a_im = mag * jnp.sin(li * step)
    den = lr * lr + li * li
    coef_re = ((a_re - 1.0) * lr + a_im * li) / den
    coef_im = (a_im * lr - (a_re - 1.0) * li) / den
    bre = b_re.astype(F32)
    bim = b_im.astype(F32)
    bb_re = coef_re[..., None] * bre - coef_im[..., None] * bim
    bb_im = coef_re[..., None] * bim + coef_im[..., None] * bre

    eye = jnp.eye(GROUPS_PER_BLOCK, dtype=F32)

    def pack_b(b):
        b4 = b.reshape(N_CBLOCKS, GROUPS_PER_BLOCK, SSM_STATE, SSM_GROUP).transpose(0, 1, 3, 2)
        full = b4[:, :, :, None, :] * eye[None, :, None, :, None]
        return full.reshape(N_CBLOCKS, LANES, BLOCK_STATE)

    def pack_c(c):
        c4 = c.reshape(N_CBLOCKS, GROUPS_PER_BLOCK, SSM_GROUP, SSM_STATE).transpose(0, 1, 3, 2)
        full = c4[:, :, :, None, :] * eye[None, :, None, :, None]
        return full.reshape(N_CBLOCKS, BLOCK_STATE, LANES)

    return {
        "b": jnp.concatenate([pack_b(bb_re), pack_b(bb_im)], axis=-1).astype(BF16),
        "cre": pack_c(c_re.astype(F32)).astype(BF16),
        "cim": pack_c(-c_im.astype(F32)).astype(BF16),
        "are": jnp.broadcast_to(a_re.reshape(1, STATE_COLS), (batch, STATE_COLS)),
        "aim": jnp.broadcast_to(a_im.reshape(1, STATE_COLS), (batch, STATE_COLS)),
        "d": d_skip.astype(F32).reshape(1, D_MODEL),
        "wglu": w_glu.astype(BF16),
    }


def _kv_prompt_kernel(x_ref, g_ref, w_ref, kp_ref, vtp_ref, k_ref, v_ref):
    i = pl.program_id(1)

    @pl.when(i == 0)
    def _():
        kp_ref[...] = jnp.zeros(kp_ref.shape, BF16)
        vtp_ref[...] = jnp.zeros(vtp_ref.shape, BF16)

    @pl.when(i > 0)
    def _():
        kv = _dot(_rms(x_ref[0], g_ref[...]).astype(BF16), w_ref[...])
        kp_ref[0] = kv[:, :D_MODEL].astype(BF16)
        vtp_ref[0] = kv[:, D_MODEL:].T.astype(BF16)

        @pl.when(i == pl.num_programs(1) - 1)
        def _():
            k_ref[0] = kv[:, :D_MODEL]
            v_ref[0] = kv[:, D_MODEL:]


def _kv_prompt(x, g, w):
    n, l, _ = x.shape
    tiles = l // BAND_PAST
    padded = jax.ShapeDtypeStruct((n, BAND_PAST + l, D_MODEL), BF16)
    padded_t = jax.ShapeDtypeStruct((n, D_MODEL, BAND_PAST + l), BF16)
    tail = jax.ShapeDtypeStruct((n, BAND_PAST, D_MODEL), F32)
    row_block = pl.BlockSpec((1, BAND_PAST, D_MODEL), lambda b, i: (b, i, 0))
    col_block = pl.BlockSpec((1, D_MODEL, BAND_PAST), lambda b, i: (b, 0, i))
    tail_block = pl.BlockSpec((1, BAND_PAST, D_MODEL), lambda b, i: (b, 0, 0))
    return pl.pallas_call(
        _kv_prompt_kernel,
        out_shape=(padded, padded_t, tail, tail),
        grid=(n, tiles + 1),
        in_specs=[
            pl.BlockSpec((1, BAND_PAST, D_MODEL), lambda b, i: (b, jnp.maximum(i - 1, 0), 0)),
            _const_spec(g.shape),
            _const_spec(w.shape),
        ],
        out_specs=(row_block, col_block, tail_block, tail_block),
        compiler_params=pltpu.CompilerParams(
            dimension_semantics=("arbitrary", "arbitrary"), vmem_limit_bytes=VMEM_LIMIT),
        name="kv_prompt",
    )(x, g, w)


def _kv_sample_kernel(x_ref, g_ref, w_ref, k_ref, v_ref):
    kv = _dot(_rms(x_ref[...], g_ref[...]).astype(BF16), w_ref[...])
    k_ref[...] = kv[:, :D_MODEL]
    v_ref[...] = kv[:, D_MODEL:]


def _kv_sample(x, g, w):
    out = jax.ShapeDtypeStruct(x.shape, F32)
    return pl.pallas_call(
        _kv_sample_kernel,
        out_shape=(out, out),
        compiler_params=pltpu.CompilerParams(vmem_limit_bytes=VMEM_LIMIT),
        name="kv_sample",
    )(x, g, w)


def _attn_prompt_kernel(x_ref, g_ref, k_ref, vt_ref, wq_ref, wo_ref, base_ref, o_ref,
                        bias_scr, ot_scr, q_scr, s_scr, p_scr, *, q_rows, k_rows):
    b = pl.program_id(0)
    j = pl.program_id(1)
    width = base_ref.shape[1]
    n_kb = k_rows // KEY_BLOCK
    n_pairs = HEAD_PAIRS

    @pl.when((b == 0) & (j == 0))
    def _():
        kj = lax.broadcasted_iota(jnp.int32, (k_rows, q_rows), 0)
        qi = lax.broadcasted_iota(jnp.int32, (k_rows, q_rows), 1)
        lo = (qi // CHUNK) * CHUNK
        in_band = (kj >= lo) & (kj < lo + BAND)
        for h in range(N_HEADS):
            rows = jnp.broadcast_to(base_ref[h:h + 1, :], (k_rows, width))
            toeplitz = pltpu.roll(rows, 0, 1, stride=1, stride_axis=0)
            bias_scr[h] = jnp.where(in_band, toeplitz[:, :q_rows] * LOG2E, NEG_INF)

    x = x_ref[0]
    xn = _rms(x, g_ref[2:3, :]).astype(BF16)
    q_scr[...] = (_dot(xn, wq_ref[...]) * (SCALE * LOG2E)).astype(BF16)
    row0 = pl.multiple_of(j * q_rows, q_rows)
    lane = lax.broadcasted_iota(jnp.int32, (1, LANES), 1)
    zero = jnp.zeros((), BF16)

    def scores(hp, masked):
        lanes = pl.ds(pl.multiple_of(hp * LANES, LANES), LANES)
        q2 = q_scr[:, lanes]
        maxima = []
        for half in range(2):
            h = 2 * hp + half
            qh = jnp.where((lane < HEAD_DIM) == (half == 0), q2, zero)
            k2 = k_ref[0, pl.ds(row0, k_rows), lanes]
            st = _dot_nt(k2, qh) + bias_scr[h]
            if masked:
                key_row = row0 + lax.broadcasted_iota(jnp.int32, st.shape, 0)
                st = jnp.where(key_row >= BAND_PAST, st, NEG_INF)
            s_scr[half] = st
            maxima.append(jnp.max(st, axis=0, keepdims=True))
        return tuple(maxima)

    def probs(maxima):
        for half in range(2):
            for kb in range(n_kb):
                rows = pl.ds(kb * KEY_BLOCK, KEY_BLOCK)
                p_scr[half, rows, :] = jnp.exp2(s_scr[half, rows, :] - maxima[half]).astype(BF16)

    ones_rows = jnp.ones((2 * SUBLANES, k_rows), BF16)

    def weighted_values(hp):
        for half in range(2):
            h = 2 * hp + half
            rows = pl.ds(pl.multiple_of(h * HEAD_DIM, HEAD_DIM), HEAD_DIM)
            vt = jnp.concatenate([vt_ref[0, rows, pl.ds(row0, k_rows)], ones_rows], axis=0)
            ot = _dot(vt, p_scr[half])
            ot_scr[rows, :] = ot[:HEAD_DIM] * (1.0 / ot[HEAD_DIM:HEAD_DIM + 1])

    def heads(masked):
        m = scores(0, masked)
        probs(m)
        m = scores(1, masked)

        def body(t, m_prev):
            weighted_values(t - 2)
            probs(m_prev)
            return scores(t, masked)

        m = lax.fori_loop(2, n_pairs, body, m)
        weighted_values(n_pairs - 2)
        probs(m)
        weighted_values(n_pairs - 1)

    needs_mask = row0 < BAND_PAST

    @pl.when(needs_mask)
    def _():
        heads(True)

    @pl.when(jnp.logical_not(needs_mask))
    def _():
        heads(False)

    y = _dot(ot_scr[...].T.astype(BF16), wo_ref[...])
    o_ref[0] = x + _rms(y, g_ref[3:4, :])


def _bias_base(table, q_rows, k_rows):
    width = -(-(q_rows + k_rows) // LANES) * LANES
    d = np.zeros((width,), np.int64)
    d[:q_rows] = np.arange(q_rows)
    d[width - (k_rows - 1):] = -np.arange(k_rows - 1, 0, -1)
    idx = np.clip(d + BAND_PAST, -MAX_REL, MAX_REL) + MAX_REL
    return jnp.take(table.astype(F32), jnp.asarray(idx, jnp.int32), axis=1)


def _attn_prompt(x, gains, kp, vtp, wq, wo, table):
    n, l, _ = x.shape
    q_rows = min(ATT_QROWS, l)
    k_rows = q_rows + BAND_PAST
    base = _bias_base(table, q_rows, k_rows)
    return pl.pallas_call(
        functools.partial(_attn_prompt_kernel, q_rows=q_rows, k_rows=k_rows),
        out_shape=jax.ShapeDtypeStruct(x.shape, F32),
        grid=(n, l // q_rows),
        in_specs=[
            pl.BlockSpec((1, q_rows, D_MODEL), lambda b, j: (b, j, 0)),
            _const_spec(gains.shape),
            pl.BlockSpec((1, kp.shape[1], D_MODEL), lambda b, j: (b, 0, 0)),
            pl.BlockSpec((1, D_MODEL, vtp.shape[2]), lambda b, j: (b, 0, 0)),
            _const_spec(wq.shape), _const_spec(wo.shape), _const_spec(base.shape),
        ],
        out_specs=pl.BlockSpec((1, q_rows, D_MODEL), lambda b, j: (b, j, 0)),
        scratch_shapes=[
            pltpu.VMEM((N_HEADS, k_rows, q_rows), F32),
            pltpu.VMEM((D_MODEL, q_rows), F32),
            pltpu.VMEM((q_rows, D_MODEL), BF16),
            pltpu.VMEM((2, k_rows, q_rows), F32),
            pltpu.VMEM((2, k_rows, q_rows), BF16),
        ],
        compiler_params=pltpu.CompilerParams(
            dimension_semantics=("arbitrary", "arbitrary"), vmem_limit_bytes=VMEM_LIMIT),
        name="attn_prompt",
    )(x, gains, kp, vtp, wq, wo, base)


def _attn_sample_kernel(x_ref, g_ref, ck_ref, cv_ref, nk_ref, nv_ref, wq_ref, wo_ref, bc_ref, bn_ref,
                        o_ref, o_scr):
    x = x_ref[0]
    xn = _rms(x, g_ref[2:3, :]).astype(BF16)
    q = (_dot(xn, wq_ref[...]) * SCALE).astype(BF16)
    first_head = lax.broadcasted_iota(jnp.int32, (1, LANES), 1) < HEAD_DIM
    zero = jnp.zeros((), BF16)
    for hp in range(HEAD_PAIRS):
        lanes = slice(hp * LANES, (hp + 1) * LANES)
        q2 = q[:, lanes]
        ck = ck_ref[0, :, lanes].astype(BF16)
        cv = cv_ref[0, :, lanes].astype(BF16)
        nk = nk_ref[0, :, lanes].astype(BF16)
        nv = nv_ref[0, :, lanes].astype(BF16)
        outs = []
        for half, qh in enumerate((jnp.where(first_head, q2, zero), jnp.where(first_head, zero, q2))):
            h = 2 * hp + half
            sc = _dot_nt(qh, ck) + bc_ref[h]
            sn = _dot_nt(qh, nk) + bn_ref[h]
            m = jnp.maximum(jnp.max(sc, axis=-1, keepdims=True), jnp.max(sn, axis=-1, keepdims=True))
            pc = jnp.exp(sc - m)
            pn = jnp.exp(sn - m)
            denom = jnp.sum(pc, axis=-1, keepdims=True) + jnp.sum(pn, axis=-1, keepdims=True)
            outs.append((_dot(pc.astype(BF16), cv) + _dot(pn.astype(BF16), nv)) / denom)
        o_scr[:, lanes] = jnp.where(first_head, outs[0], outs[1])
    y = _dot(o_scr[...].astype(BF16), wo_ref[...])
    o_ref[0] = x + _rms(y, g_ref[3:4, :])


def _attn_sample(x, gains, ck, cv, nk, nv, wq, wo, bias_c, bias_n):
    n, s, _ = x.shape
    r = ck.shape[1]
    seq_block = pl.BlockSpec((1, s, D_MODEL), lambda b: (b, 0, 0))
    cache_block = pl.BlockSpec((1, r, D_MODEL), lambda b: (b, 0, 0))
    return pl.pallas_call(
        _attn_sample_kernel,
        out_shape=jax.ShapeDtypeStruct(x.shape, F32),
        grid=(n,),
        in_specs=[seq_block, _const_spec(gains.shape), cache_block, cache_block, seq_block, seq_block,
                  _const_spec(wq.shape), _const_spec(wo.shape),
                  _const_spec(bias_c.shape), _const_spec(bias_n.shape)],
        out_specs=seq_block,
        scratch_shapes=[pltpu.VMEM((s, D_MODEL), F32)],
        compiler_params=pltpu.CompilerParams(
            dimension_semantics=("arbitrary",), vmem_limit_bytes=VMEM_LIMIT),
        name="attn_sample",
    )(x, gains, ck, cv, nk, nv, wq, wo, bias_c, bias_n)


def _sample_bias(table, s, r):
    q_pos = PAST_LEN + np.arange(s)
    k_pos = np.concatenate([PAST_LEN - r + np.arange(r), q_pos])
    rel = np.clip(q_pos[:, None] - k_pos[None, :], -MAX_REL, MAX_REL) + MAX_REL
    qc = q_pos // CHUNK
    kc = k_pos // CHUNK
    valid = (kc[None, :] <= qc[:, None]) & (kc[None, :] >= qc[:, None] - N_LEFT_CHUNKS)
    bias = jnp.where(valid[None], table.astype(F32)[:, rel], NEG_INF)
    return bias[:, :, :r], bias[:, :, r:]


def kernel(x_prompt, x_sample, state_ssm_re, state_ssm_im, cache_k, cache_v, norm_gains, ffn1_w_in, ffn1_w_out, ffn2_w_in, ffn2_w_out, ssm_lambda_re, ssm_lambda_im, ssm_log_step, ssm_b_re, ssm_b_im, ssm_c_re, ssm_c_im, ssm_d, ssm_w_glu, kv_norm, w_kv, attn_w_q, attn_w_o, attn_rel_bias):
    bp, lp, _ = x_prompt.shape
    bs, ls, _ = x_sample.shape
    cache_rows = cache_k.shape[1]
    assert bp == SUBLANES and bs == SUBLANES and lp % BAND_PAST == 0

    gains = norm_gains.astype(F32)
    kv_g = kv_norm.astype(F32).reshape(1, D_MODEL)
    w_kv_b = w_kv.astype(BF16)
    w1_in, w1_out = ffn1_w_in.astype(F32), ffn1_w_out.astype(F32)
    w2_in, w2_out = ffn2_w_in.astype(F32), ffn2_w_out.astype(F32)
    wq_b = [attn_w_q[i].astype(BF16) for i in range(DEPTH - N_A_LAYERS)]
    wo_b = [attn_w_o[i].astype(BF16) for i in range(DEPTH - N_A_LAYERS)]
    s5 = [_s5_params(ssm_lambda_re[a], ssm_lambda_im[a], ssm_log_step[a], ssm_b_re[a], ssm_b_im[a],
                     ssm_c_re[a], ssm_c_im[a], ssm_d[a], ssm_w_glu[a], SUBLANES)
          for a in range(N_A_LAYERS)]

    def s5_layers(x, h0_re, h0_im):
        states_re, states_im = [], []
        for a in range(N_A_LAYERS):
            x = _half_ffn(x, gains, w1_in, w1_out, a, 0, 1)
            x, hr, hi = _s5_layer(x, gains[a], s5[a], h0_re[a], h0_im[a], SUBLANES)
            x = _half_ffn(x, gains, w2_in, w2_out, a, 4, 5)
            states_re.append(hr.reshape(SUBLANES, N_GROUPS, SSM_STATE))
            states_im.append(hi.reshape(SUBLANES, N_GROUPS, SSM_STATE))
        return x, jnp.stack(states_re), jnp.stack(states_im)

    zeros = jnp.zeros((N_A_LAYERS, bp, STATE_COLS), F32)
    xt = x_prompt.astype(F32).transpose(1, 0, 2).reshape(lp * bp, D_MODEL)
    xt, ssm_re_p, ssm_im_p = s5_layers(xt, zeros, zeros)
    xb = xt.reshape(lp, bp, D_MODEL).transpose(1, 0, 2)
    kp, vtp, k_tail, v_tail = _kv_prompt(xb, kv_g, w_kv_b)
    for b in range(DEPTH - N_A_LAYERS):
        layer = N_A_LAYERS + b
        flat = _half_ffn(xb.reshape(bp * lp, D_MODEL), gains, w1_in, w1_out, layer, 0, 1)
        xb = _attn_prompt(flat.reshape(bp, lp, D_MODEL), gains[layer], kp, vtp, wq_b[b], wo_b[b],
                          attn_rel_bias[b])
        flat = _half_ffn(xb.reshape(bp * lp, D_MODEL), gains, w2_in, w2_out, layer, 4, 5)
        xb = flat.reshape(bp, lp, D_MODEL)
    y_prompt = xb
    k_prompt = k_tail.reshape(bp, BAND_PAST, N_HEADS, HEAD_DIM)
    v_prompt = v_tail.reshape(bp, BAND_PAST, N_HEADS, HEAD_DIM)

    h0_re = state_ssm_re.astype(F32).reshape(N_A_LAYERS, bs, STATE_COLS)
    h0_im = state_ssm_im.astype(F32).reshape(N_A_LAYERS, bs, STATE_COLS)
    st = x_sample.astype(F32).transpose(1, 0, 2).reshape(ls * bs, D_MODEL)
    st, ssm_re_s, ssm_im_s = s5_layers(st, h0_re, h0_im)
    sb = st.reshape(ls, bs, D_MODEL).transpose(1, 0, 2).reshape(bs * ls, D_MODEL)
    k_new, v_new = _kv_sample(sb, kv_g, w_kv_b)
    k_new = k_new.reshape(bs, ls, D_MODEL)
    v_new = v_new.reshape(bs, ls, D_MODEL)
    ck = cache_k.astype(F32).reshape(bs, cache_rows, D_MODEL)
    cv = cache_v.astype(F32).reshape(bs, cache_rows, D_MODEL)
    for b in range(DEPTH - N_A_LAYERS):
        layer = N_A_LAYERS + b
        sb = _half_ffn(sb, gains, w1_in, w1_out, layer, 0, 1)
        bias_c, bias_n = _sample_bias(attn_rel_bias[b], ls, cache_rows)
        sb = _attn_sample(sb.reshape(bs, ls, D_MODEL), gains[layer], ck, cv, k_new, v_new,
                          wq_b[b], wo_b[b], bias_c, bias_n).reshape(bs * ls, D_MODEL)
        sb = _half_ffn(sb, gains, w2_in, w2_out, layer, 4, 5)
    y_sample = sb.reshape(bs, ls, D_MODEL)
    k_sample = k_new.reshape(bs, ls, N_HEADS, HEAD_DIM)
    v_sample = v_new.reshape(bs, ls, N_HEADS, HEAD_DIM)

    return (y_prompt, y_sample, ssm_re_p, ssm_im_p, k_prompt, v_prompt,
            ssm_re_s, ssm_im_s, k_sample, v_sample)
```

```python
import functools
import math

import numpy as np
import jax
import jax.numpy as jnp
from jax import lax
from jax.experimental import pallas as pl
from jax.experimental.pallas import tpu as pltpu

D_MODEL = 1024
DEPTH = 4
PAST_LEN = 4096
CHUNK = 64
N_A_LAYERS = DEPTH // 2
D_FF = 2816
SSM_GROUP = 16
N_GROUPS = D_MODEL // SSM_GROUP
SSM_STATE = 64
N_HEADS = 16
HEAD_DIM = D_MODEL // N_HEADS
N_LEFT_CHUNKS = 8
BAND_PAST = N_LEFT_CHUNKS * CHUNK
BAND = (N_LEFT_CHUNKS + 1) * CHUNK
MAX_REL = 128
EPS = 1e-6
SCALE = HEAD_DIM ** -0.5
NEG_INF = -1e30

F32 = jnp.float32
BF16 = jnp.bfloat16

LANES = 128
SUBLANES = 8
VMEM_LIMIT = 56 * 1024 * 1024

FFN_ROWS = 512
FFN_CHUNK = 256
S5_STEPS = 64
S5_COLS = 1024
GROUPS_PER_BLOCK = LANES // SSM_GROUP
N_CBLOCKS = D_MODEL // LANES
STATE_COLS = N_GROUPS * SSM_STATE
BLOCK_STATE = GROUPS_PER_BLOCK * SSM_STATE
ATT_QROWS = 256
KEY_BLOCK = 128
HEAD_PAIRS = N_HEADS // 2
LOG2E = 1.4426950408889634


def _const_spec(shape):
    zeros = (0,) * len(shape)
    return pl.BlockSpec(shape, lambda *_: zeros, pipeline_mode=pl.Buffered(1))


def _rms(x, g):
    return x * lax.rsqrt(jnp.mean(x * x, axis=-1, keepdims=True) + EPS) * g


def _sigmoid(x):
    return 1.0 / (1.0 + jnp.exp(-x))


def _gelu_tanh(x):
    c = math.sqrt(2.0 / math.pi)
    return 0.5 * x * (1.0 + jnp.tanh(c * (x + 0.044715 * (x * x * x))))


def _dot(a, b):
    return jnp.dot(a, b, preferred_element_type=F32)


def _dot_nt(a, b):
    return lax.dot_general(a, b, (((1,), (1,)), ((), ())), preferred_element_type=F32)


def _ffn_kernel(xp_ref, xs_ref, g_ref, w_in_ref, w_out_ref, op_ref, os_ref, *, pre, post, prompt_tiles):
    i = pl.program_id(0)

    def half_step(x_ref, o_ref):
        x = x_ref[...]
        xn = _rms(x, g_ref[pre:pre + 1, :]).astype(BF16)
        acc = jnp.zeros(x.shape, F32)
        for c in range(D_FF // FFN_CHUNK):
            lo = c * FFN_CHUNK
            gate = _dot(xn, w_in_ref[:, lo:lo + FFN_CHUNK].astype(BF16))
            up = _dot(xn, w_in_ref[:, D_FF + lo:D_FF + lo + FFN_CHUNK].astype(BF16))
            act = (gate * _sigmoid(gate) * up).astype(BF16)
            acc = acc + _dot(act, w_out_ref[lo:lo + FFN_CHUNK, :].astype(BF16))
        o_ref[...] = x + 0.5 * _rms(acc, g_ref[post:post + 1, :])

    @pl.when(i < prompt_tiles)
    def _():
        half_step(xp_ref, op_ref)

    @pl.when(i == prompt_tiles)
    def _():
        half_step(xs_ref, os_ref)


def _layer_spec(w, layer):
    return pl.BlockSpec((None,) + w.shape[1:], lambda *_: (layer,) + (0,) * (w.ndim - 1),
                        pipeline_mode=pl.Buffered(1))


def _half_ffn(xp, xs, gains, w_in, w_out, layer, pre, post):
    tiles = xp.shape[0] // FFN_ROWS
    prompt_block = pl.BlockSpec((FFN_ROWS, D_MODEL), lambda i: (jnp.minimum(i, tiles - 1), 0))
    sample_block = pl.BlockSpec(xs.shape, lambda i: (0, 0))
    return pl.pallas_call(
        functools.partial(_ffn_kernel, pre=pre, post=post, prompt_tiles=tiles),
        out_shape=(jax.ShapeDtypeStruct(xp.shape, F32), jax.ShapeDtypeStruct(xs.shape, F32)),
        grid=(tiles + 1,),
        in_specs=[
            prompt_block,
            sample_block,
            _layer_spec(gains, layer),
            _layer_spec(w_in, layer),
            _layer_spec(w_out, layer),
        ],
        out_specs=(prompt_block, sample_block),
        compiler_params=pltpu.CompilerParams(
            dimension_semantics=("arbitrary",), vmem_limit_bytes=VMEM_LIMIT),
        name="half_ffn",
    )(xp, xs, gains, w_in, w_out)


def _s5_kernel(x_ref, g_ref, b_ref, cre_ref, cim_ref, are_ref, aim_ref,
               h0re_ref, h0im_ref, d_ref, wglu_ref,
               o_ref, hre_out, him_out,
               bure, buim, hre, him, y_scr, *, steps, batch):
    i = pl.program_id(0)

    @pl.when(i == 0)
    def _():
        hre[...] = h0re_ref[...]
        him[...] = h0im_ref[...]

    x = x_ref[...]
    xn = _rms(x, g_ref[2:3, :])
    xb = xn.astype(BF16)

    for k in range(N_CBLOCKS):
        bu = _dot(xb[:, k * LANES:(k + 1) * LANES], b_ref[k])
        bure[:, k * BLOCK_STATE:(k + 1) * BLOCK_STATE] = bu[:, :BLOCK_STATE]
        buim[:, k * BLOCK_STATE:(k + 1) * BLOCK_STATE] = bu[:, BLOCK_STATE:]

    for cb in range(STATE_COLS // S5_COLS):
        cols = slice(cb * S5_COLS, (cb + 1) * S5_COLS)
        ar = are_ref[:, cols]
        ai = aim_ref[:, cols]

        def step(t, carry, cols=cols, ar=ar, ai=ai):
            hr, hi = carry
            rows = pl.ds(pl.multiple_of(t * batch, batch), batch)
            nhr = ar * hr - ai * hi + bure[rows, cols]
            nhi = ar * hi + ai * hr + buim[rows, cols]
            bure[rows, cols] = nhr
            buim[rows, cols] = nhi
            return nhr, nhi

        hr, hi = lax.fori_loop(0, steps, step, (hre[:, cols], him[:, cols]), unroll=2)
        hre[:, cols] = hr
        him[:, cols] = hi

    hre_out[...] = hre[...]
    him_out[...] = him[...]

    for k in range(N_CBLOCKS):
        sc = slice(k * BLOCK_STATE, (k + 1) * BLOCK_STATE)
        yk = _dot(bure[:, sc].astype(BF16), cre_ref[k]) + _dot(buim[:, sc].astype(BF16), cim_ref[k])
        y_scr[:, k * LANES:(k + 1) * LANES] = yk
    y = _gelu_tanh(y_scr[...] + d_ref[...] * xn)
    z = _dot(y.astype(BF16), wglu_ref[...])
    out = z[:, :D_MODEL] * _sigmoid(z[:, D_MODEL:])
    o_ref[...] = x + _rms(out, g_ref[3:4, :])


def _s5_layer(x, gains, p, h0re, h0im, batch):
    rows = x.shape[0]
    steps = min(S5_STEPS, rows // batch)
    tr = steps * batch
    consts = [gains, p["b"], p["cre"], p["cim"], p["are"], p["aim"],
              h0re, h0im, p["d"], p["wglu"]]
    state = jax.ShapeDtypeStruct((batch, STATE_COLS), F32)
    return pl.pallas_call(
        functools.partial(_s5_kernel, steps=steps, batch=batch),
        out_shape=(jax.ShapeDtypeStruct(x.shape, F32), state, state),
        grid=(rows // tr,),
        in_specs=[pl.BlockSpec((tr, D_MODEL), lambda i: (i, 0))] + [_const_spec(c.shape) for c in consts],
        out_specs=(pl.BlockSpec((tr, D_MODEL), lambda i: (i, 0)),
                   pl.BlockSpec((batch, STATE_COLS), lambda i: (0, 0)),
                   pl.BlockSpec((batch, STATE_COLS), lambda i: (0, 0))),
        scratch_shapes=[
            pltpu.VMEM((tr, STATE_COLS), F32),
            pltpu.VMEM((tr, STATE_COLS), F32),
            pltpu.VMEM((batch, STATE_COLS), F32),
            pltpu.VMEM((batch, STATE_COLS), F32),
            pltpu.VMEM((tr, D_MODEL), F32),
        ],
        compiler_params=pltpu.CompilerParams(
            dimension_semantics=("arbitrary",), vmem_limit_bytes=VMEM_LIMIT),
        name="s5_mixer",
    )(x, *consts)


def _s5_params(lam_re, lam_im, log_step, b_re, b_im, c_re, c_im, d_skip, w_glu, batch):
    step = jnp.exp(log_step.astype(F32))[:, None]
    lr = lam_re.astype(F32)
    li = lam_im.astype(F32)
    mag = jnp.exp(lr * step)
    a_re = mag * jnp.cos(li * step)
    a_im = mag * jnp.sin(li * step)
    den = lr * lr + li * li
    coef_re = ((a_re - 1.0) * lr + a_im * li) / den
    coef_im = (a_im * lr - (a_re - 1.0) * li) / den
    bre = b_re.astype(F32)
    bim = b_im.astype(F32)
    bb_re = coef_re[..., None] * bre - coef_im[..., None] * bim
    bb_im = coef_re[..., None] * bim + coef_im[..., None] * bre

    eye = jnp.eye(GROUPS_PER_BLOCK, dtype=F32)

    def pack_b(b):
        b4 = b.reshape(N_CBLOCKS, GROUPS_PER_BLOCK, SSM_STATE, SSM_GROUP).transpose(0, 1, 3, 2)
        full = b4[:, :, :, None, :] * eye[None, :, None, :, None]
        return full.reshape(N_CBLOCKS, LANES, BLOCK_STATE)

    def pack_c(c):
        c4 = c.reshape(N_CBLOCKS, GROUPS_PER_BLOCK, SSM_GROUP, SSM_STATE).transpose(0, 1, 3, 2)
        full = c4[:, :, :, None, :] * eye[None, :, None, :, None]
        return full.reshape(N_CBLOCKS, BLOCK_STATE, LANES)

    return {
        "b": jnp.concatenate([pack_b(bb_re), pack_b(bb_im)], axis=-1).astype(BF16),
        "cre": pack_c(c_re.astype(F32)).astype(BF16),
        "cim": pack_c(-c_im.astype(F32)).astype(BF16),
        "are": jnp.broadcast_to(a_re.reshape(1, STATE_COLS), (batch, STATE_COLS)),
        "aim": jnp.broadcast_to(a_im.reshape(1, STATE_COLS), (batch, STATE_COLS)),
        "d": d_skip.astype(F32).reshape(1, D_MODEL),
        "wglu": w_glu.astype(BF16),
    }


def _kv_prompt_kernel(x_ref, g_ref, w_ref, kp_ref, vtp_ref, k_ref, v_ref):
    i = pl.program_id(1)

    @pl.when(i == 0)
    def _():
        kp_ref[...] = jnp.zeros(kp_ref.shape, BF16)
        vtp_ref[...] = jnp.zeros(vtp_ref.shape, BF16)

    @pl.when(i > 0)
    def _():
        kv = _dot(_rms(x_ref[0], g_ref[...]).astype(BF16), w_ref[...])
        kp_ref[0] = kv[:, :D_MODEL].astype(BF16)
        vtp_ref[0] = kv[:, D_MODEL:].T.astype(BF16)

        @pl.when(i == pl.num_programs(1) - 1)
        def _():
            k_ref[0] = kv[:, :D_MODEL]
            v_ref[0] = kv[:, D_MODEL:]


def _kv_prompt(x, g, w):
    n, l, _ = x.shape
    tiles = l // BAND_PAST
    padded = jax.ShapeDtypeStruct((n, BAND_PAST + l, D_MODEL), BF16)
    padded_t = jax.ShapeDtypeStruct((n, D_MODEL, BAND_PAST + l), BF16)
    tail = jax.ShapeDtypeStruct((n, BAND_PAST, D_MODEL), F32)
    row_block = pl.BlockSpec((1, BAND_PAST, D_MODEL), lambda b, i: (b, i, 0))
    col_block = pl.BlockSpec((1, D_MODEL, BAND_PAST), lambda b, i: (b, 0, i))
    tail_block = pl.BlockSpec((1, BAND_PAST, D_MODEL), lambda b, i: (b, 0, 0))
    return pl.pallas_call(
        _kv_prompt_kernel,
        out_shape=(padded, padded_t, tail, tail),
        grid=(n, tiles + 1),
        in_specs=[
            pl.BlockSpec((1, BAND_PAST, D_MODEL), lambda b, i: (b, jnp.maximum(i - 1, 0), 0)),
            _const_spec(g.shape),
            _const_spec(w.shape),
        ],
        out_specs=(row_block, col_block, tail_block, tail_block),
        compiler_params=pltpu.CompilerParams(
            dimension_semantics=("arbitrary", "arbitrary"), vmem_limit_bytes=VMEM_LIMIT),
        name="kv_prompt",
    )(x, g, w)


def _kv_sample_kernel(x_ref, g_ref, w_ref, k_ref, v_ref):
    kv = _dot(_rms(x_ref[...], g_ref[...]).astype(BF16), w_ref[...])
    k_ref[...] = kv[:, :D_MODEL]
    v_ref[...] = kv[:, D_MODEL:]


def _kv_sample(x, g, w):
    out = jax.ShapeDtypeStruct(x.shape, F32)
    return pl.pallas_call(
        _kv_sample_kernel,
        out_shape=(out, out),
        compiler_params=pltpu.CompilerParams(vmem_limit_bytes=VMEM_LIMIT),
        name="kv_sample",
    )(x, g, w)


def _attn_prompt_kernel(x_ref, g_ref, k_ref, vt_ref, wq_ref, wo_ref, base_ref, o_ref,
                        bias_scr, ot_scr, q_scr, s_scr, p_scr, *, q_rows, k_rows):
    b = pl.program_id(0)
    j = pl.program_id(1)
    width = base_ref.shape[1]
    n_kb = k_rows // KEY_BLOCK
    n_pairs = HEAD_PAIRS

    @pl.when((b == 0) & (j == 0))
    def _():
        kj = lax.broadcasted_iota(jnp.int32, (k_rows, q_rows), 0)
        qi = lax.broadcasted_iota(jnp.int32, (k_rows, q_rows), 1)
        lo = (qi // CHUNK) * CHUNK
        in_band = (kj >= lo) & (kj < lo + BAND)
        for h in range(N_HEADS):
            rows = jnp.broadcast_to(base_ref[h:h + 1, :], (k_rows, width))
            toeplitz = pltpu.roll(rows, 0, 1, stride=1, stride_axis=0)
            bias_scr[h] = jnp.where(in_band, toeplitz[:, :q_rows] * LOG2E, NEG_INF)

    x = x_ref[0]
    xn = _rms(x, g_ref[2:3, :]).astype(BF16)
    q_scr[...] = (_dot(xn, wq_ref[...]) * (SCALE * LOG2E)).astype(BF16)
    row0 = pl.multiple_of(j * q_rows, q_rows)
    lane = lax.broadcasted_iota(jnp.int32, (1, LANES), 1)
    zero = jnp.zeros((), BF16)

    def scores(hp, masked):
        lanes = pl.ds(pl.multiple_of(hp * LANES, LANES), LANES)
        q2 = q_scr[:, lanes]
        maxima = []
        for half in range(2):
            h = 2 * hp + half
            qh = jnp.where((lane < HEAD_DIM) == (half == 0), q2, zero)
            k2 = k_ref[0, pl.ds(row0, k_rows), lanes]
            st = _dot_nt(k2, qh) + bias_scr[h]
            if masked:
                key_row = row0 + lax.broadcasted_iota(jnp.int32, st.shape, 0)
                st = jnp.where(key_row >= BAND_PAST, st, NEG_INF)
            s_scr[half] = st
            maxima.append(jnp.max(st, axis=0, keepdims=True))
        return tuple(maxima)

    def probs(maxima):
        for half in range(2):
            for kb in range(n_kb):
                rows = pl.ds(kb * KEY_BLOCK, KEY_BLOCK)
                p_scr[half, rows, :] = jnp.exp2(s_scr[half, rows, :] - maxima[half]).astype(BF16)

    ones_rows = jnp.ones((2 * SUBLANES, k_rows), BF16)

    def weighted_values(hp):
        for half in range(2):
            h = 2 * hp + half
            rows = pl.ds(pl.multiple_of(h * HEAD_DIM, HEAD_DIM), HEAD_DIM)
            vt = jnp.concatenate([vt_ref[0, rows, pl.ds(row0, k_rows)], ones_rows], axis=0)
            ot = _dot(vt, p_scr[half])
            ot_scr[rows, :] = ot[:HEAD_DIM] * (1.0 / ot[HEAD_DIM:HEAD_DIM + 1])

    def heads(masked):
        m = scores(0, masked)
        probs(m)
        m = scores(1, masked)

        def body(t, m_prev):
            weighted_values(t - 2)
            probs(m_prev)
            return scores(t, masked)

        m = lax.fori_loop(2, n_pairs, body, m)
        weighted_values(n_pairs - 2)
        probs(m)
        weighted_values(n_pairs - 1)

    needs_mask = row0 < BAND_PAST

    @pl.when(needs_mask)
    def _():
        heads(True)

    @pl.when(jnp.logical_not(needs_mask))
    def _():
        heads(False)

    y = _dot(ot_scr[...].T.astype(BF16), wo_ref[...])
    o_ref[0] = x + _rms(y, g_ref[3:4, :])


def _bias_base(table, q_rows, k_rows):
    width = -(-(q_rows + k_rows) // LANES) * LANES
    d = np.zeros((width,), np.int64)
    d[:q_rows] = np.arange(q_rows)
    d[width - (k_rows - 1):] = -np.arange(k_rows - 1, 0, -1)
    idx = np.clip(d + BAND_PAST, -MAX_REL, MAX_REL) + MAX_REL
    return jnp.take(table.astype(F32), jnp.asarray(idx, jnp.int32), axis=1)


def _attn_prompt(x, gains, kp, vtp, wq, wo, table):
    n, l, _ = x.shape
    q_rows = min(ATT_QROWS, l)
    k_rows = q_rows + BAND_PAST
    base = _bias_base(table, q_rows, k_rows)
    return pl.pallas_call(
        functools.partial(_attn_prompt_kernel, q_rows=q_rows, k_rows=k_rows),
        out_shape=jax.ShapeDtypeStruct(x.shape, F32),
        grid=(n, l // q_rows),
        in_specs=[
            pl.BlockSpec((1, q_rows, D_MODEL), lambda b, j: (b, j, 0)),
            _const_spec(gains.shape),
            pl.BlockSpec((1, kp.shape[1], D_MODEL), lambda b, j: (b, 0, 0)),
            pl.BlockSpec((1, D_MODEL, vtp.shape[2]), lambda b, j: (b, 0, 0)),
            _const_spec(wq.shape), _const_spec(wo.shape), _const_spec(base.shape),
        ],
        out_specs=pl.BlockSpec((1, q_rows, D_MODEL), lambda b, j: (b, j, 0)),
        scratch_shapes=[
            pltpu.VMEM((N_HEADS, k_rows, q_rows), F32),
            pltpu.VMEM((D_MODEL, q_rows), F32),
            pltpu.VMEM((q_rows, D_MODEL), BF16),
            pltpu.VMEM((2, k_rows, q_rows), F32),
            pltpu.VMEM((2, k_rows, q_rows), BF16),
        ],
        compiler_params=pltpu.CompilerParams(
            dimension_semantics=("arbitrary", "arbitrary"), vmem_limit_bytes=VMEM_LIMIT),
        name="attn_prompt",
    )(x, gains, kp, vtp, wq, wo, base)


def _attn_sample_kernel(x_ref, g_ref, ck_ref, cv_ref, nk_ref, nv_ref, wq_ref, wo_ref, bc_ref, bn_ref,
                        o_ref, o_scr):
    x = x_ref[0]
    xn = _rms(x, g_ref[2:3, :]).astype(BF16)
    q = (_dot(xn, wq_ref[...]) * SCALE).astype(BF16)
    first_head = lax.broadcasted_iota(jnp.int32, (1, LANES), 1) < HEAD_DIM
    zero = jnp.zeros((), BF16)
    for hp in range(HEAD_PAIRS):
        lanes = slice(hp * LANES, (hp + 1) * LANES)
        q2 = q[:, lanes]
        ck = ck_ref[0, :, lanes].astype(BF16)
        cv = cv_ref[0, :, lanes].astype(BF16)
        nk = nk_ref[0, :, lanes].astype(BF16)
        nv = nv_ref[0, :, lanes].astype(BF16)
        outs = []
        for half, qh in enumerate((jnp.where(first_head, q2, zero), jnp.where(first_head, zero, q2))):
            h = 2 * hp + half
            sc = _dot_nt(qh, ck) + bc_ref[h]
            sn = _dot_nt(qh, nk) + bn_ref[h]
            m = jnp.maximum(jnp.max(sc, axis=-1, keepdims=True), jnp.max(sn, axis=-1, keepdims=True))
            pc = jnp.exp(sc - m)
            pn = jnp.exp(sn - m)
            denom = jnp.sum(pc, axis=-1, keepdims=True) + jnp.sum(pn, axis=-1, keepdims=True)
            outs.append((_dot(pc.astype(BF16), cv) + _dot(pn.astype(BF16), nv)) / denom)
        o_scr[:, lanes] = jnp.where(first_head, outs[0], outs[1])
    y = _dot(o_scr[...].astype(BF16), wo_ref[...])
    o_ref[0] = x + _rms(y, g_ref[3:4, :])


def _attn_sample(x, gains, ck, cv, nk, nv, wq, wo, bias_c, bias_n):
    n, s, _ = x.shape
    r = ck.shape[1]
    seq_block = pl.BlockSpec((1, s, D_MODEL), lambda b: (b, 0, 0))
    cache_block = pl.BlockSpec((1, r, D_MODEL), lambda b: (b, 0, 0))
    return pl.pallas_call(
        _attn_sample_kernel,
        out_shape=jax.ShapeDtypeStruct(x.shape, F32),
        grid=(n,),
        in_specs=[seq_block, _const_spec(gains.shape), cache_block, cache_block, seq_block, seq_block,
                  _const_spec(wq.shape), _const_spec(wo.shape),
                  _const_spec(bias_c.shape), _const_spec(bias_n.shape)],
        out_specs=seq_block,
        scratch_shapes=[pltpu.VMEM((s, D_MODEL), F32)],
        compiler_params=pltpu.CompilerParams(
            dimension_semantics=("arbitrary",), vmem_limit_bytes=VMEM_LIMIT),
        name="attn_sample",
    )(x, gains, ck, cv, nk, nv, wq, wo, bias_c, bias_n)


def _sample_bias(table, s, r):
    q_pos = PAST_LEN + np.arange(s)
    k_pos = np.concatenate([PAST_LEN - r + np.arange(r), q_pos])
    rel = np.clip(q_pos[:, None] - k_pos[None, :], -MAX_REL, MAX_REL) + MAX_REL
    qc = q_pos // CHUNK
    kc = k_pos // CHUNK
    valid = (kc[None, :] <= qc[:, None]) & (kc[None, :] >= qc[:, None] - N_LEFT_CHUNKS)
    bias = jnp.where(valid[None], table.astype(F32)[:, rel], NEG_INF)
    return bias[:, :, :r], bias[:, :, r:]


def kernel(x_prompt, x_sample, state_ssm_re, state_ssm_im, cache_k, cache_v, norm_gains, ffn1_w_in, ffn1_w_out, ffn2_w_in, ffn2_w_out, ssm_lambda_re, ssm_lambda_im, ssm_log_step, ssm_b_re, ssm_b_im, ssm_c_re, ssm_c_im, ssm_d, ssm_w_glu, kv_norm, w_kv, attn_w_q, attn_w_o, attn_rel_bias):
    bp, lp, _ = x_prompt.shape
    bs, ls, _ = x_sample.shape
    cache_rows = cache_k.shape[1]
    assert bp == SUBLANES and bs == SUBLANES and lp % BAND_PAST == 0

    gains = norm_gains.astype(F32)
    kv_g = kv_norm.astype(F32).reshape(1, D_MODEL)
    w_kv_b = w_kv.astype(BF16)
    w1_in, w1_out = ffn1_w_in.astype(F32), ffn1_w_out.astype(F32)
    w2_in, w2_out = ffn2_w_in.astype(F32), ffn2_w_out.astype(F32)
    wq_b = [attn_w_q[i].astype(BF16) for i in range(DEPTH - N_A_LAYERS)]
    wo_b = [attn_w_o[i].astype(BF16) for i in range(DEPTH - N_A_LAYERS)]
    s5 = [_s5_params(ssm_lambda_re[a], ssm_lambda_im[a], ssm_log_step[a], ssm_b_re[a], ssm_b_im[a],
                     ssm_c_re[a], ssm_c_im[a], ssm_d[a], ssm_w_glu[a], SUBLANES)
          for a in range(N_A_LAYERS)]

    xp = x_prompt.astype(F32).transpose(1, 0, 2).reshape(lp * bp, D_MODEL)
    xs = x_sample.astype(F32).transpose(1, 0, 2).reshape(ls * bs, D_MODEL)
    zeros = jnp.zeros((bp, STATE_COLS), F32)
    h0_re = state_ssm_re.astype(F32).reshape(N_A_LAYERS, bs, STATE_COLS)
    h0_im = state_ssm_im.astype(F32).reshape(N_A_LAYERS, bs, STATE_COLS)
    states = {"p_re": [], "p_im": [], "s_re": [], "s_im": []}
    for a in range(N_A_LAYERS):
        xp, xs = _half_ffn(xp, xs, gains, w1_in, w1_out, a, 0, 1)
        xp, pr, pi = _s5_layer(xp, gains[a], s5[a], zeros, zeros, bp)
        xs, sr, si = _s5_layer(xs, gains[a], s5[a], h0_re[a], h0_im[a], bs)
        xp, xs = _half_ffn(xp, xs, gains, w2_in, w2_out, a, 4, 5)
        for key, val in (("p_re", pr), ("p_im", pi), ("s_re", sr), ("s_im", si)):
            states[key].append(val.reshape(SUBLANES, N_GROUPS, SSM_STATE))

    xp = xp.reshape(lp, bp, D_MODEL).transpose(1, 0, 2)
    xs = xs.reshape(ls, bs, D_MODEL).transpose(1, 0, 2).reshape(bs * ls, D_MODEL)
    kp, vtp, k_tail, v_tail = _kv_prompt(xp, kv_g, w_kv_b)
    k_new, v_new = _kv_sample(xs, kv_g, w_kv_b)
    k_new = k_new.reshape(bs, ls, D_MODEL)
    v_new = v_new.reshape(bs, ls, D_MODEL)
    ck = cache_k.astype(F32).reshape(bs, cache_rows, D_MODEL)
    cv = cache_v.astype(F32).reshape(bs, cache_rows, D_MODEL)
    xp = xp.reshape(bp * lp, D_MODEL)
    for b in range(DEPTH - N_A_LAYERS):
        layer = N_A_LAYERS + b
        xp, xs = _half_ffn(xp, xs, gains, w1_in, w1_out, layer, 0, 1)
        xp = _attn_prompt(xp.reshape(bp, lp, D_MODEL), gains[layer], kp, vtp, wq_b[b], wo_b[b],
                          attn_rel_bias[b]).reshape(bp * lp, D_MODEL)
        bias_c, bias_n = _sample_bias(attn_rel_bias[b], ls, cache_rows)
        xs = _attn_sample(xs.reshape(bs, ls, D_MODEL), gains[layer], ck, cv, k_new, v_new,
                          wq_b[b], wo_b[b], bias_c, bias_n).reshape(bs * ls, D_MODEL)
        xp, xs = _half_ffn(xp, xs, gains, w2_in, w2_out, layer, 4, 5)

    return (xp.reshape(bp, lp, D_MODEL), xs.reshape(bs, ls, D_MODEL),
            jnp.stack(states["p_re"]), jnp.stack(states["p_im"]),
            k_tail.reshape(bp, BAND_PAST, N_HEADS, HEAD_DIM), v_tail.reshape(bp, BAND_PAST, N_HEADS, HEAD_DIM),
            jnp.stack(states["s_re"]), jnp.stack(states["s_im"]),
            k_new.reshape(bs, ls, N_HEADS, HEAD_DIM), v_new.reshape(bs, ls, N_HEADS, HEAD_DIM))
```

```python
import functools
import math

import numpy as np
import jax
import jax.numpy as jnp
from jax import lax
from jax.experimental import pallas as pl
from jax.experimental.pallas import tpu as pltpu

D_MODEL = 1024
DEPTH = 4
PAST_LEN = 4096
CHUNK = 64
N_A_LAYERS = DEPTH // 2
D_FF = 2816
SSM_GROUP = 16
N_GROUPS = D_MODEL // SSM_GROUP
SSM_STATE = 64
N_HEADS = 16
HEAD_DIM = D_MODEL // N_HEADS
N_LEFT_CHUNKS = 8
BAND_PAST = N_LEFT_CHUNKS * CHUNK
BAND = (N_LEFT_CHUNKS + 1) * CHUNK
MAX_REL = 128
EPS = 1e-6
SCALE = HEAD_DIM ** -0.5
NEG_INF = -1e30

F32 = jnp.float32
BF16 = jnp.bfloat16

LANES = 128
SUBLANES = 8
VMEM_LIMIT = 56 * 1024 * 1024

FFN_ROWS = 512
FFN_CHUNK = 256
S5_STEPS = 64
S5_COLS = 1024
GROUPS_PER_BLOCK = LANES // SSM_GROUP
N_CBLOCKS = D_MODEL // LANES
STATE_COLS = N_GROUPS * SSM_STATE
BLOCK_STATE = GROUPS_PER_BLOCK * SSM_STATE
ATT_QROWS = 256
KEY_BLOCK = 128
HEAD_PAIRS = N_HEADS // 2
LOG2E = 1.4426950408889634


def _const_spec(shape):
    zeros = (0,) * len(shape)
    return pl.BlockSpec(shape, lambda *_: zeros, pipeline_mode=pl.Buffered(1))


def _rms(x, g):
    return x * lax.rsqrt(jnp.mean(x * x, axis=-1, keepdims=True) + EPS) * g


def _sigmoid(x):
    return 1.0 / (1.0 + jnp.exp(-x))


def _gelu_tanh(x):
    c = math.sqrt(2.0 / math.pi)
    return 0.5 * x * (1.0 + jnp.tanh(c * (x + 0.044715 * (x * x * x))))


def _dot(a, b):
    return jnp.dot(a, b, preferred_element_type=F32)


def _dot_nt(a, b):
    return lax.dot_general(a, b, (((1,), (1,)), ((), ())), preferred_element_type=F32)


def _ffn_kernel(xp_ref, xs_ref, g_ref, w_in_ref, w_out_ref, op_ref, os_ref, *, pre, post, prompt_tiles):
    i = pl.program_id(0)

    def half_step(x_ref, o_ref):
        x = x_ref[...]
        xn = _rms(x, g_ref[pre:pre + 1, :]).astype(BF16)
        acc = jnp.zeros(x.shape, F32)
        for c in range(D_FF // FFN_CHUNK):
            lo = c * FFN_CHUNK
            gate = _dot(xn, w_in_ref[:, lo:lo + FFN_CHUNK].astype(BF16))
            up = _dot(xn, w_in_ref[:, D_FF + lo:D_FF + lo + FFN_CHUNK].astype(BF16))
            act = (gate * _sigmoid(gate) * up).astype(BF16)
            acc = acc + _dot(act, w_out_ref[lo:lo + FFN_CHUNK, :].astype(BF16))
        o_ref[...] = x + 0.5 * _rms(acc, g_ref[post:post + 1, :])

    @pl.when(i < prompt_tiles)
    def _():
        half_step(xp_ref, op_ref)

    @pl.when(i == prompt_tiles)
    def _():
        half_step(xs_ref, os_ref)


def _layer_spec(w, layer):
    return pl.BlockSpec((None,) + w.shape[1:], lambda *_: (layer,) + (0,) * (w.ndim - 1),
                        pipeline_mode=pl.Buffered(1))


def _half_ffn(xp, xs, gains, w_in, w_out, layer, pre, post):
    tiles = xp.shape[0] // FFN_ROWS
    prompt_block = pl.BlockSpec((FFN_ROWS, D_MODEL), lambda i: (jnp.minimum(i, tiles - 1), 0))
    sample_block = pl.BlockSpec(xs.shape, lambda i: (0, 0))
    return pl.pallas_call(
        functools.partial(_ffn_kernel, pre=pre, post=post, prompt_tiles=tiles),
        out_shape=(jax.ShapeDtypeStruct(xp.shape, F32), jax.ShapeDtypeStruct(xs.shape, F32)),
        grid=(tiles + 1,),
        in_specs=[
            prompt_block,
            sample_block,
            _layer_spec(gains, layer),
            _layer_spec(w_in, layer),
            _layer_spec(w_out, layer),
        ],
        out_specs=(prompt_block, sample_block),
        compiler_params=pltpu.CompilerParams(
            dimension_semantics=("arbitrary",), vmem_limit_bytes=VMEM_LIMIT),
        name="half_ffn",
    )(xp, xs, gains, w_in, w_out)


def _s5_kernel(x_ref, g_ref, b_ref, cre_ref, cim_ref, are_ref, aim_ref,
               h0re_ref, h0im_ref, d_ref, wglu_ref,
               o_ref, hre_out, him_out,
               bure, buim, hre, him, y_scr, *, steps, batch):
    i = pl.program_id(0)

    @pl.when(i == 0)
    def _():
        hre[...] = h0re_ref[...]
        him[...] = h0im_ref[...]

    x = x_ref[...]
    xn = _rms(x, g_ref[2:3, :])
    xb = xn.astype(BF16)

    for k in range(N_CBLOCKS):
        bu = _dot(xb[:, k * LANES:(k + 1) * LANES], b_ref[k])
        bure[:, k * BLOCK_STATE:(k + 1) * BLOCK_STATE] = bu[:, :BLOCK_STATE]
        buim[:, k * BLOCK_STATE:(k + 1) * BLOCK_STATE] = bu[:, BLOCK_STATE:]

    blocks_per_scan = S5_COLS // BLOCK_STATE
    for cb in range(STATE_COLS // S5_COLS):
        cols = slice(cb * S5_COLS, (cb + 1) * S5_COLS)
        ar = are_ref[:, cols]
        ai = aim_ref[:, cols]
        hr = hre[:, cols]
        hi = him[:, cols]
        for t in range(steps):
            rows = slice(t * batch, (t + 1) * batch)
            hr, hi = ar * hr - ai * hi + bure[rows, cols], ar * hi + ai * hr + buim[rows, cols]
            bure[rows, cols] = hr
            buim[rows, cols] = hi
        hre[:, cols] = hr
        him[:, cols] = hi
        hre_out[:, cols] = hr
        him_out[:, cols] = hi
        for k in range(cb * blocks_per_scan, (cb + 1) * blocks_per_scan):
            sc = slice(k * BLOCK_STATE, (k + 1) * BLOCK_STATE)
            yk = _dot(bure[:, sc].astype(BF16), cre_ref[k]) + _dot(buim[:, sc].astype(BF16), cim_ref[k])
            y_scr[:, k * LANES:(k + 1) * LANES] = yk
    y = _gelu_tanh(y_scr[...] + d_ref[...] * xn)
    z = _dot(y.astype(BF16), wglu_ref[...])
    out = z[:, :D_MODEL] * _sigmoid(z[:, D_MODEL:])
    o_ref[...] = x + _rms(out, g_ref[3:4, :])


def _s5_layer(x, gains, p, h0re, h0im, batch):
    rows = x.shape[0]
    steps = min(S5_STEPS, rows // batch)
    tr = steps * batch
    consts = [gains, p["b"], p["cre"], p["cim"], p["are"], p["aim"],
              h0re, h0im, p["d"], p["wglu"]]
    state = jax.ShapeDtypeStruct((batch, STATE_COLS), F32)
    return pl.pallas_call(
        functools.partial(_s5_kernel, steps=steps, batch=batch),
        out_shape=(jax.ShapeDtypeStruct(x.shape, F32), state, state),
        grid=(rows // tr,),
        in_specs=[pl.BlockSpec((tr, D_MODEL), lambda i: (i, 0))] + [_const_spec(c.shape) for c in consts],
        out_specs=(pl.BlockSpec((tr, D_MODEL), lambda i: (i, 0)),
                   pl.BlockSpec((batch, STATE_COLS), lambda i: (0, 0)),
                   pl.BlockSpec((batch, STATE_COLS), lambda i: (0, 0))),
        scratch_shapes=[
            pltpu.VMEM((tr, STATE_COLS), F32),
            pltpu.VMEM((tr, STATE_COLS), F32),
            pltpu.VMEM((batch, STATE_COLS), F32),
            pltpu.VMEM((batch, STATE_COLS), F32),
            pltpu.VMEM((tr, D_MODEL), F32),
        ],
        compiler_params=pltpu.CompilerParams(
            dimension_semantics=("arbitrary",), vmem_limit_bytes=VMEM_LIMIT),
        name="s5_mixer",
    )(x, *consts)


def _s5_params(lam_re, lam_im, log_step, b_re, b_im, c_re, c_im, d_skip, w_glu, batch):
    step = jnp.exp(log_step.astype(F32))[:, None]
    lr = lam_re.astype(F32)
    li = lam_im.astype(F32)
    mag = jnp.exp(lr * step)
    a_re = mag * jnp.cos(li * step)
    a_im = mag * jnp.sin(li * step)
    den = lr * lr + li * li
    coef_re = ((a_re - 1.0) * lr + a_im * li) / den
    coef_im = (a_im * lr - (a_re - 1.0) * li) / den
    bre = b_re.astype(F32)
    bim = b_im.astype(F32)
    bb_re = coef_re[..., None] * bre - coef_im[..., None] * bim
    bb_im = coef_re[..., None] * bim + coef_im[..., None] * bre

    eye = jnp.eye(GROUPS_PER_BLOCK, dtype=F32)

    def pack_b(b):
        b4 = b.reshape(N_CBLOCKS, GROUPS_PER_BLOCK, SSM_STATE, SSM_GROUP).transpose(0, 1, 3, 2)
        full = b4[:, :, :, None, :] * eye[None, :, None, :, None]
        return full.reshape(N_CBLOCKS, LANES, BLOCK_STATE)

    def pack_c(c):
        c4 = c.reshape(N_CBLOCKS, GROUPS_PER_BLOCK, SSM_GROUP, SSM_STATE).transpose(0, 1, 3, 2)
        full = c4[:, :, :, None, :] * eye[None, :, None, :, None]
        return full.reshape(N_CBLOCKS, BLOCK_STATE, LANES)

    return {
        "b": jnp.concatenate([pack_b(bb_re), pack_b(bb_im)], axis=-1).astype(BF16),
        "cre": pack_c(c_re.astype(F32)).astype(BF16),
        "cim": pack_c(-c_im.astype(F32)).astype(BF16),
        "are": jnp.broadcast_to(a_re.reshape(1, STATE_COLS), (batch, STATE_COLS)),
        "aim": jnp.broadcast_to(a_im.reshape(1, STATE_COLS), (batch, STATE_COLS)),
        "d": d_skip.astype(F32).reshape(1, D_MODEL),
        "wglu": w_glu.astype(BF16),
    }


def _kv_prompt_kernel(x_ref, g_ref, w_ref, kp_ref, vtp_ref, k_ref, v_ref):
    i = pl.program_id(1)

    @pl.when(i == 0)
    def _():
        kp_ref[...] = jnp.zeros(kp_ref.shape, BF16)
        vtp_ref[...] = jnp.zeros(vtp_ref.shape, BF16)

    @pl.when(i > 0)
    def _():
        kv = _dot(_rms(x_ref[0], g_ref[...]).astype(BF16), w_ref[...])
        kp_ref[0] = kv[:, :D_MODEL].astype(BF16)
        vtp_ref[0] = kv[:, D_MODEL:].T.astype(BF16)

        @pl.when(i == pl.num_programs(1) - 1)
        def _():
            k_ref[0] = kv[:, :D_MODEL]
            v_ref[0] = kv[:, D_MODEL:]


def _kv_prompt(x, g, w):
    n, l, _ = x.shape
    tiles = l // BAND_PAST
    padded = jax.ShapeDtypeStruct((n, BAND_PAST + l, D_MODEL), BF16)
    padded_t = jax.ShapeDtypeStruct((n, D_MODEL, BAND_PAST + l), BF16)
    tail = jax.ShapeDtypeStruct((n, BAND_PAST, D_MODEL), F32)
    row_block = pl.BlockSpec((1, BAND_PAST, D_MODEL), lambda b, i: (b, i, 0))
    col_block = pl.BlockSpec((1, D_MODEL, BAND_PAST), lambda b, i: (b, 0, i))
    tail_block = pl.BlockSpec((1, BAND_PAST, D_MODEL), lambda b, i: (b, 0, 0))
    return pl.pallas_call(
        _kv_prompt_kernel,
        out_shape=(padded, padded_t, tail, tail),
        grid=(n, tiles + 1),
        in_specs=[
            pl.BlockSpec((1, BAND_PAST, D_MODEL), lambda b, i: (b, jnp.maximum(i - 1, 0), 0)),
            _const_spec(g.shape),
            _const_spec(w.shape),
        ],
        out_specs=(row_block, col_block, tail_block, tail_block),
        compiler_params=pltpu.CompilerParams(
            dimension_semantics=("arbitrary", "arbitrary"), vmem_limit_bytes=VMEM_LIMIT),
        name="kv_prompt",
    )(x, g, w)


def _kv_sample_kernel(x_ref, g_ref, w_ref, k_ref, v_ref):
    kv = _dot(_rms(x_ref[...], g_ref[...]).astype(BF16), w_ref[...])
    k_ref[...] = kv[:, :D_MODEL]
    v_ref[...] = kv[:, D_MODEL:]


def _kv_sample(x, g, w):
    out = jax.ShapeDtypeStruct(x.shape, F32)
    return pl.pallas_call(
        _kv_sample_kernel,
        out_shape=(out, out),
        compiler_params=pltpu.CompilerParams(vmem_limit_bytes=VMEM_LIMIT),
        name="kv_sample",
    )(x, g, w)


def _attn_prompt_kernel(x_ref, g_ref, k_ref, vt_ref, wq_ref, wo_ref, base_ref, o_ref,
                        bias_scr, ot_scr, q_scr, s_scr, p_scr, *, q_rows, k_rows):
    b = pl.program_id(0)
    j = pl.program_id(1)
    width = base_ref.shape[1]
    n_kb = k_rows // KEY_BLOCK
    n_pairs = HEAD_PAIRS

    @pl.when((b == 0) & (j == 0))
    def _():
        kj = lax.broadcasted_iota(jnp.int32, (k_rows, q_rows), 0)
        qi = lax.broadcasted_iota(jnp.int32, (k_rows, q_rows), 1)
        lo = (qi // CHUNK) * CHUNK
        in_band = (kj >= lo) & (kj < lo + BAND)
        for h in range(N_HEADS):
            rows = jnp.broadcast_to(base_ref[h:h + 1, :], (k_rows, width))
            toeplitz = pltpu.roll(rows, 0, 1, stride=1, stride_axis=0)
            bias_scr[h] = jnp.where(in_band, toeplitz[:, :q_rows] * LOG2E, NEG_INF)

    x = x_ref[0]
    xn = _rms(x, g_ref[2:3, :]).astype(BF16)
    q_scr[...] = (_dot(xn, wq_ref[...]) * (SCALE * LOG2E)).astype(BF16)
    row0 = pl.multiple_of(j * q_rows, q_rows)
    lane = lax.broadcasted_iota(jnp.int32, (1, LANES), 1)
    zero = jnp.zeros((), BF16)

    def scores(hp, masked):
        lanes = pl.ds(pl.multiple_of(hp * LANES, LANES), LANES)
        q2 = q_scr[:, lanes]
        maxima = []
        for half in range(2):
            h = 2 * hp + half
            qh = jnp.where((lane < HEAD_DIM) == (half == 0), q2, zero)
            k2 = k_ref[0, pl.ds(row0, k_rows), lanes]
            st = _dot_nt(k2, qh) + bias_scr[h]
            if masked:
                key_row = row0 + lax.broadcasted_iota(jnp.int32, st.shape, 0)
                st = jnp.where(key_row >= BAND_PAST, st, NEG_INF)
            s_scr[half] = st
            maxima.append(jnp.max(st, axis=0, keepdims=True))
        return tuple(maxima)

    def probs(maxima):
        for half in range(2):
            for kb in range(n_kb):
                rows = pl.ds(kb * KEY_BLOCK, KEY_BLOCK)
                p_scr[half, rows, :] = jnp.exp2(s_scr[half, rows, :] - maxima[half]).astype(BF16)

    ones_rows = jnp.ones((2 * SUBLANES, k_rows), BF16)

    def weighted_values(hp):
        for half in range(2):
            h = 2 * hp + half
            rows = pl.ds(pl.multiple_of(h * HEAD_DIM, HEAD_DIM), HEAD_DIM)
            vt = jnp.concatenate([vt_ref[0, rows, pl.ds(row0, k_rows)], ones_rows], axis=0)
            ot = _dot(vt, p_scr[half])
            ot_scr[rows, :] = ot[:HEAD_DIM] * (1.0 / ot[HEAD_DIM:HEAD_DIM + 1])

    def heads(masked):
        m = scores(0, masked)
        probs(m)
        m = scores(1, masked)

        def body(t, m_prev):
            weighted_values(t - 2)
            probs(m_prev)
            return scores(t, masked)

        m = lax.fori_loop(2, n_pairs, body, m)
        weighted_values(n_pairs - 2)
        probs(m)
        weighted_values(n_pairs - 1)

    needs_mask = row0 < BAND_PAST

    @pl.when(needs_mask)
    def _():
        heads(True)

    @pl.when(jnp.logical_not(needs_mask))
    def _():
        heads(False)

    y = _dot(ot_scr[...].T.astype(BF16), wo_ref[...])
    o_ref[0] = x + _rms(y, g_ref[3:4, :])


def _bias_base(table, q_rows, k_rows):
    width = -(-(q_rows + k_rows) // LANES) * LANES
    d = np.zeros((width,), np.int64)
    d[:q_rows] = np.arange(q_rows)
    d[width - (k_rows - 1):] = -np.arange(k_rows - 1, 0, -1)
    idx = np.clip(d + BAND_PAST, -MAX_REL, MAX_REL) + MAX_REL
    return jnp.take(table.astype(F32), jnp.asarray(idx, jnp.int32), axis=1)


def _attn_prompt(x, gains, kp, vtp, wq, wo, table):
    n, l, _ = x.shape
    q_rows = min(ATT_QROWS, l)
    k_rows = q_rows + BAND_PAST
    base = _bias_base(table, q_rows, k_rows)
    return pl.pallas_call(
        functools.partial(_attn_prompt_kernel, q_rows=q_rows, k_rows=k_rows),
        out_shape=jax.ShapeDtypeStruct(x.shape, F32),
        grid=(n, l // q_rows),
        in_specs=[
            pl.BlockSpec((1, q_rows, D_MODEL), lambda b, j: (b, j, 0)),
            _const_spec(gains.shape),
            pl.BlockSpec((1, kp.shape[1], D_MODEL), lambda b, j: (b, 0, 0)),
            pl.BlockSpec((1, D_MODEL, vtp.shape[2]), lambda b, j: (b, 0, 0)),
            _const_spec(wq.shape), _const_spec(wo.shape), _const_spec(base.shape),
        ],
        out_specs=pl.BlockSpec((1, q_rows, D_MODEL), lambda b, j: (b, j, 0)),
        scratch_shapes=[
            pltpu.VMEM((N_HEADS, k_rows, q_rows), F32),
            pltpu.VMEM((D_MODEL, q_rows), F32),
            pltpu.VMEM((q_rows, D_MODEL), BF16),
            pltpu.VMEM((2, k_rows, q_rows), F32),
            pltpu.VMEM((2, k_rows, q_rows), BF16),
        ],
        compiler_params=pltpu.CompilerParams(
            dimension_semantics=("arbitrary", "arbitrary"), vmem_limit_bytes=VMEM_LIMIT),
        name="attn_prompt",
    )(x, gains, kp, vtp, wq, wo, base)


def _attn_sample_kernel(x_ref, g_ref, ck_ref, cv_ref, nk_ref, nv_ref, wq_ref, wo_ref, bc_ref, bn_ref,
                        o_ref, o_scr):
    x = x_ref[0]
    xn = _rms(x, g_ref[2:3, :]).astype(BF16)
    q = (_dot(xn, wq_ref[...]) * SCALE).astype(BF16)
    first_head = lax.broadcasted_iota(jnp.int32, (1, LANES), 1) < HEAD_DIM
    zero = jnp.zeros((), BF16)
    for hp in range(HEAD_PAIRS):
        lanes = slice(hp * LANES, (hp + 1) * LANES)
        q2 = q[:, lanes]
        ck = ck_ref[0, :, lanes].astype(BF16)
        cv = cv_ref[0, :, lanes].astype(BF16)
        nk = nk_ref[0, :, lanes].astype(BF16)
        nv = nv_ref[0, :, lanes].astype(BF16)
        outs = []
        for half, qh in enumerate((jnp.where(first_head, q2, zero), jnp.where(first_head, zero, q2))):
            h = 2 * hp + half
            sc = _dot_nt(qh, ck) + bc_ref[h]
            sn = _dot_nt(qh, nk) + bn_ref[h]
            m = jnp.maximum(jnp.max(sc, axis=-1, keepdims=True), jnp.max(sn, axis=-1, keepdims=True))
            pc = jnp.exp(sc - m)
            pn = jnp.exp(sn - m)
            denom = jnp.sum(pc, axis=-1, keepdims=True) + jnp.sum(pn, axis=-1, keepdims=True)
            outs.append((_dot(pc.astype(BF16), cv) + _dot(pn.astype(BF16), nv)) / denom)
        o_scr[:, lanes] = jnp.where(first_head, outs[0], outs[1])
    y = _dot(o_scr[...].astype(BF16), wo_ref[...])
    o_ref[0] = x + _rms(y, g_ref[3:4, :])


def _attn_sample(x, gains, ck, cv, nk, nv, wq, wo, bias_c, bias_n):
    n, s, _ = x.shape
    r = ck.shape[1]
    seq_block = pl.BlockSpec((1, s, D_MODEL), lambda b: (b, 0, 0))
    cache_block = pl.BlockSpec((1, r, D_MODEL), lambda b: (b, 0, 0))
    return pl.pallas_call(
        _attn_sample_kernel,
        out_shape=jax.ShapeDtypeStruct(x.shape, F32),
        grid=(n,),
        in_specs=[seq_block, _const_spec(gains.shape), cache_block, cache_block, seq_block, seq_block,
                  _const_spec(wq.shape), _const_spec(wo.shape),
                  _const_spec(bias_c.shape), _const_spec(bias_n.shape)],
        out_specs=seq_block,
        scratch_shapes=[pltpu.VMEM((s, D_MODEL), F32)],
        compiler_params=pltpu.CompilerParams(
            dimension_semantics=("arbitrary",), vmem_limit_bytes=VMEM_LIMIT),
        name="attn_sample",
    )(x, gains, ck, cv, nk, nv, wq, wo, bias_c, bias_n)


def _sample_bias(table, s, r):
    q_pos = PAST_LEN + np.arange(s)
    k_pos = np.concatenate([PAST_LEN - r + np.arange(r), q_pos])
    rel = np.clip(q_pos[:, None] - k_pos[None, :], -MAX_REL, MAX_REL) + MAX_REL
    qc = q_pos // CHUNK
    kc = k_pos // CHUNK
    valid = (kc[None, :] <= qc[:, None]) & (kc[None, :] >= qc[:, None] - N_LEFT_CHUNKS)
    bias = jnp.where(valid[None], table.astype(F32)[:, rel], NEG_INF)
    return bias[:, :, :r], bias[:, :, r:]


def kernel(x_prompt, x_sample, state_ssm_re, state_ssm_im, cache_k, cache_v, norm_gains, ffn1_w_in, ffn1_w_out, ffn2_w_in, ffn2_w_out, ssm_lambda_re, ssm_lambda_im, ssm_log_step, ssm_b_re, ssm_b_im, ssm_c_re, ssm_c_im, ssm_d, ssm_w_glu, kv_norm, w_kv, attn_w_q, attn_w_o, attn_rel_bias):
    bp, lp, _ = x_prompt.shape
    bs, ls, _ = x_sample.shape
    cache_rows = cache_k.shape[1]
    assert bp == SUBLANES and bs == SUBLANES and lp % BAND_PAST == 0

    gains = norm_gains.astype(F32)
    kv_g = kv_norm.astype(F32).reshape(1, D_MODEL)
    w_kv_b = w_kv.astype(BF16)
    w1_in, w1_out = ffn1_w_in.astype(F32), ffn1_w_out.astype(F32)
    w2_in, w2_out = ffn2_w_in.astype(F32), ffn2_w_out.astype(F32)
    wq_b = [attn_w_q[i].astype(BF16) for i in range(DEPTH - N_A_LAYERS)]
    wo_b = [attn_w_o[i].astype(BF16) for i in range(DEPTH - N_A_LAYERS)]
    s5 = [_s5_params(ssm_lambda_re[a], ssm_lambda_im[a], ssm_log_step[a], ssm_b_re[a], ssm_b_im[a],
                     ssm_c_re[a], ssm_c_im[a], ssm_d[a], ssm_w_glu[a], SUBLANES)
          for a in range(N_A_LAYERS)]

    xp = x_prompt.astype(F32).transpose(1, 0, 2).reshape(lp * bp, D_MODEL)
    xs = x_sample.astype(F32).transpose(1, 0, 2).reshape(ls * bs, D_MODEL)
    zeros = jnp.zeros((bp, STATE_COLS), F32)
    h0_re = state_ssm_re.astype(F32).reshape(N_A_LAYERS, bs, STATE_COLS)
    h0_im = state_ssm_im.astype(F32).reshape(N_A_LAYERS, bs, STATE_COLS)
    states = {"p_re": [], "p_im": [], "s_re": [], "s_im": []}
    for a in range(N_A_LAYERS):
        xp, xs = _half_ffn(xp, xs, gains, w1_in, w1_out, a, 0, 1)
        xp, pr, pi = _s5_layer(xp, gains[a], s5[a], zeros, zeros, bp)
        xs, sr, si = _s5_layer(xs, gains[a], s5[a], h0_re[a], h0_im[a], bs)
        xp, xs = _half_ffn(xp, xs, gains, w2_in, w2_out, a, 4, 5)
        for key, val in (("p_re", pr), ("p_im", pi), ("s_re", sr), ("s_im", si)):
            states[key].append(val.reshape(SUBLANES, N_GROUPS, SSM_STATE))

    xp = xp.reshape(lp, bp, D_MODEL).transpose(1, 0, 2)
    xs = xs.reshape(ls, bs, D_MODEL).transpose(1, 0, 2).reshape(bs * ls, D_MODEL)
    kp, vtp, k_tail, v_tail = _kv_prompt(xp, kv_g, w_kv_b)
    k_new, v_new = _kv_sample(xs, kv_g, w_kv_b)
    k_new = k_new.reshape(bs, ls, D_MODEL)
    v_new = v_new.reshape(bs, ls, D_MODEL)
    ck = cache_k.astype(F32).reshape(bs, cache_rows, D_MODEL)
    cv = cache_v.astype(F32).reshape(bs, cache_rows, D_MODEL)
    xp = xp.reshape(bp * lp, D_MODEL)
    for b in range(DEPTH - N_A_LAYERS):
        layer = N_A_LAYERS + b
        xp, xs = _half_ffn(xp, xs, gains, w1_in, w1_out, layer, 0, 1)
        xp = _attn_prompt(xp.reshape(bp, lp, D_MODEL), gains[layer], kp, vtp, wq_b[b], wo_b[b],
                          attn_rel_bias[b]).reshape(bp * lp, D_MODEL)
        bias_c, bias_n = _sample_bias(attn_rel_bias[b], ls, cache_rows)
        xs = _attn_sample(xs.reshape(bs, ls, D_MODEL), gains[layer], ck, cv, k_new, v_new,
                          wq_b[b], wo_b[b], bias_c, bias_n).reshape(bs * ls, D_MODEL)
        xp, xs = _half_ffn(xp, xs, gains, w2_in, w2_out, layer, 4, 5)

    return (xp.reshape(bp, lp, D_MODEL), xs.reshape(bs, ls, D_MODEL),
            jnp.stack(states["p_re"]), jnp.stack(states["p_im"]),
            k_tail.reshape(bp, BAND_PAST, N_HEADS, HEAD_DIM), v_tail.reshape(bp, BAND_PAST, N_HEADS, HEAD_DIM),
            jnp.stack(states["s_re"]), jnp.stack(states["s_im"]),
            k_new.reshape(bs, ls, N_HEADS, HEAD_DIM), v_new.reshape(bs, ls, N_HEADS, HEAD_DIM))
```

```python
import functools
import math

import numpy as np
import jax
import jax.numpy as jnp
from jax import lax
from jax.experimental import pallas as pl
from jax.experimental.pallas import tpu as pltpu

D_MODEL = 1024
DEPTH = 4
PAST_LEN = 4096
CHUNK = 64
N_A_LAYERS = DEPTH // 2
D_FF = 2816
SSM_GROUP = 16
N_GROUPS = D_MODEL // SSM_GROUP
SSM_STATE = 64
N_HEADS = 16
HEAD_DIM = D_MODEL // N_HEADS
N_LEFT_CHUNKS = 8
BAND_PAST = N_LEFT_CHUNKS * CHUNK
BAND = (N_LEFT_CHUNKS + 1) * CHUNK
MAX_REL = 128
EPS = 1e-6
SCALE = HEAD_DIM ** -0.5
NEG_INF = -1e30

F32 = jnp.float32
BF16 = jnp.bfloat16

LANES = 128
SUBLANES = 8
VMEM_LIMIT = 56 * 1024 * 1024

FFN_ROWS = 512
FFN_CHUNK = 256
S5_STEPS = 64
S5_COLS = 1024
GROUPS_PER_BLOCK = LANES // SSM_GROUP
N_CBLOCKS = D_MODEL // LANES
STATE_COLS = N_GROUPS * SSM_STATE
BLOCK_STATE = GROUPS_PER_BLOCK * SSM_STATE
ATT_QROWS = 256
KEY_BLOCK = 128
HEAD_PAIRS = N_HEADS // 2
LOG2E = 1.4426950408889634


def _const_spec(shape):
    zeros = (0,) * len(shape)
    return pl.BlockSpec(shape, lambda *_: zeros, pipeline_mode=pl.Buffered(1))


def _rms(x, g):
    return x * lax.rsqrt(jnp.mean(x * x, axis=-1, keepdims=True) + EPS) * g


def _sigmoid(x):
    return 1.0 / (1.0 + jnp.exp(-x))


def _gelu_tanh(x):
    c = math.sqrt(2.0 / math.pi)
    return 0.5 * x * (1.0 + jnp.tanh(c * (x + 0.044715 * (x * x * x))))


def _dot(a, b):
    return jnp.dot(a, b, preferred_element_type=F32)


def _dot_nt(a, b):
    return lax.dot_general(a, b, (((1,), (1,)), ((), ())), preferred_element_type=F32)


def _ffn_kernel(xp_ref, xs_ref, g_ref, w_in_ref, w_out_ref, op_ref, os_ref, *, pre, post, prompt_tiles, relayout):
    i = pl.program_id(0)

    def half_step(x_ref, o_ref):
        x = x_ref[...]
        if relayout == "b2t":
            n, s, _ = x.shape
            x = x.reshape(n * s, D_MODEL)
        xn = _rms(x, g_ref[pre:pre + 1, :]).astype(BF16)
        acc = jnp.zeros(x.shape, F32)
        for c in range(D_FF // FFN_CHUNK):
            lo = c * FFN_CHUNK
            gate = _dot(xn, w_in_ref[:, lo:lo + FFN_CHUNK].astype(BF16))
            up = _dot(xn, w_in_ref[:, D_FF + lo:D_FF + lo + FFN_CHUNK].astype(BF16))
            act = (gate * _sigmoid(gate) * up).astype(BF16)
            acc = acc + _dot(act, w_out_ref[lo:lo + FFN_CHUNK, :].astype(BF16))
        out = x + 0.5 * _rms(acc, g_ref[post:post + 1, :])
        if relayout == "b2t":
            out = jnp.swapaxes(out.reshape(n, s, D_MODEL), 0, 1).reshape(n * s, D_MODEL)
        elif relayout == "t2b":
            n, s, _ = o_ref.shape
            out = jnp.swapaxes(out.reshape(s, n, D_MODEL), 0, 1)
        o_ref[...] = out

    @pl.when(i < prompt_tiles)
    def _():
        half_step(xp_ref, op_ref)

    @pl.when(i == prompt_tiles)
    def _():
        half_step(xs_ref, os_ref)


def _layer_spec(w, layer):
    return pl.BlockSpec((None,) + w.shape[1:], lambda *_: (layer,) + (0,) * (w.ndim - 1),
                        pipeline_mode=pl.Buffered(1))


def _half_ffn(xp, xs, gains, w_in, w_out, layer, pre, post, relayout=None):
    batch = SUBLANES
    steps = FFN_ROWS // batch

    def flat(x):
        return (x.shape[0] * x.shape[1], D_MODEL) if x.ndim == 3 else x.shape

    def stacked(x):
        return (batch, x.shape[0] // batch, D_MODEL) if x.ndim == 2 else x.shape

    tiles = flat(xp)[0] // FFN_ROWS
    rows_block = pl.BlockSpec((FFN_ROWS, D_MODEL), lambda i: (jnp.minimum(i, tiles - 1), 0))
    seq_block = pl.BlockSpec((batch, steps, D_MODEL), lambda i: (0, jnp.minimum(i, tiles - 1), 0))
    in_3d, out_3d = relayout == "b2t", relayout == "t2b"
    xs_out = stacked(xs) if out_3d else flat(xs)
    xp_out = stacked(xp) if out_3d else flat(xp)
    return pl.pallas_call(
        functools.partial(_ffn_kernel, pre=pre, post=post, prompt_tiles=tiles, relayout=relayout),
        out_shape=(jax.ShapeDtypeStruct(xp_out, F32), jax.ShapeDtypeStruct(xs_out, F32)),
        grid=(tiles + 1,),
        in_specs=[
            seq_block if in_3d else rows_block,
            pl.BlockSpec(xs.shape, lambda i: (0,) * xs.ndim),
            _layer_spec(gains, layer),
            _layer_spec(w_in, layer),
            _layer_spec(w_out, layer),
        ],
        out_specs=(seq_block if out_3d else rows_block, pl.BlockSpec(xs_out, lambda i: (0,) * len(xs_out))),
        compiler_params=pltpu.CompilerParams(
            dimension_semantics=("arbitrary",), vmem_limit_bytes=VMEM_LIMIT),
        name="half_ffn",
    )(xp, xs, gains, w_in, w_out)


def _s5_kernel(x_ref, g_ref, b_ref, cre_ref, cim_ref, are_ref, aim_ref,
               h0re_ref, h0im_ref, d_ref, wglu_ref,
               o_ref, hre_out, him_out,
               bure, buim, hre, him, y_scr, *, steps, batch):
    i = pl.program_id(0)

    @pl.when(i == 0)
    def _():
        hre[...] = h0re_ref[...]
        him[...] = h0im_ref[...]

    x = x_ref[...]
    xn = _rms(x, g_ref[2:3, :])
    xb = xn.astype(BF16)

    for k in range(N_CBLOCKS):
        bu = _dot(xb[:, k * LANES:(k + 1) * LANES], b_ref[k])
        bure[:, k * BLOCK_STATE:(k + 1) * BLOCK_STATE] = bu[:, :BLOCK_STATE]
        buim[:, k * BLOCK_STATE:(k + 1) * BLOCK_STATE] = bu[:, BLOCK_STATE:]

    blocks_per_scan = S5_COLS // BLOCK_STATE
    for cb in range(STATE_COLS // S5_COLS):
        cols = slice(cb * S5_COLS, (cb + 1) * S5_COLS)
        ar = are_ref[:, cols]
        ai = aim_ref[:, cols]
        hr = hre[:, cols]
        hi = him[:, cols]
        for t in range(steps):
            rows = slice(t * batch, (t + 1) * batch)
            hr, hi = ar * hr - ai * hi + bure[rows, cols], ar * hi + ai * hr + buim[rows, cols]
            bure[rows, cols] = hr
            buim[rows, cols] = hi
        hre[:, cols] = hr
        him[:, cols] = hi
        hre_out[:, cols] = hr
        him_out[:, cols] = hi
        for k in range(cb * blocks_per_scan, (cb + 1) * blocks_per_scan):
            sc = slice(k * BLOCK_STATE, (k + 1) * BLOCK_STATE)
            yk = _dot(bure[:, sc].astype(BF16), cre_ref[k]) + _dot(buim[:, sc].astype(BF16), cim_ref[k])
            y_scr[:, k * LANES:(k + 1) * LANES] = yk
    y = _gelu_tanh(y_scr[...] + d_ref[...] * xn)
    z = _dot(y.astype(BF16), wglu_ref[...])
    out = z[:, :D_MODEL] * _sigmoid(z[:, D_MODEL:])
    o_ref[...] = x + _rms(out, g_ref[3:4, :])


def _s5_layer(x, gains, p, h0re, h0im, batch):
    rows = x.shape[0]
    steps = min(S5_STEPS, rows // batch)
    tr = steps * batch
    consts = [gains, p["b"], p["cre"], p["cim"], p["are"], p["aim"],
              h0re, h0im, p["d"], p["wglu"]]
    state = jax.ShapeDtypeStruct((batch, STATE_COLS), F32)
    return pl.pallas_call(
        functools.partial(_s5_kernel, steps=steps, batch=batch),
        out_shape=(jax.ShapeDtypeStruct(x.shape, F32), state, state),
        grid=(rows // tr,),
        in_specs=[pl.BlockSpec((tr, D_MODEL), lambda i: (i, 0))] + [_const_spec(c.shape) for c in consts],
        out_specs=(pl.BlockSpec((tr, D_MODEL), lambda i: (i, 0)),
                   pl.BlockSpec((batch, STATE_COLS), lambda i: (0, 0)),
                   pl.BlockSpec((batch, STATE_COLS), lambda i: (0, 0))),
        scratch_shapes=[
            pltpu.VMEM((tr, STATE_COLS), F32),
            pltpu.VMEM((tr, STATE_COLS), F32),
            pltpu.VMEM((batch, STATE_COLS), F32),
            pltpu.VMEM((batch, STATE_COLS), F32),
            pltpu.VMEM((tr, D_MODEL), F32),
        ],
        compiler_params=pltpu.CompilerParams(
            dimension_semantics=("arbitrary",), vmem_limit_bytes=VMEM_LIMIT),
        name="s5_mixer",
    )(x, *consts)


def _s5_params(lam_re, lam_im, log_step, b_re, b_im, c_re, c_im, d_skip, w_glu, batch):
    step = jnp.exp(log_step.astype(F32))[:, None]
    lr = lam_re.astype(F32)
    li = lam_im.astype(F32)
    mag = jnp.exp(lr * step)
    a_re = mag * jnp.cos(li * step)
    a_im = mag * jnp.sin(li * step)
    den = lr * lr + li * li
    coef_re = ((a_re - 1.0) * lr + a_im * li) / den
    coef_im = (a_im * lr - (a_re - 1.0) * li) / den
    bre = b_re.astype(F32)
    bim = b_im.astype(F32)
    bb_re = coef_re[..., None] * bre - coef_im[..., None] * bim
    bb_im = coef_re[..., None] * bim + coef_im[..., None] * bre

    eye = jnp.eye(GROUPS_PER_BLOCK, dtype=F32)

    def pack_b(b):
        b4 = b.reshape(N_CBLOCKS, GROUPS_PER_BLOCK, SSM_STATE, SSM_GROUP).transpose(0, 1, 3, 2)
        full = b4[:, :, :, None, :] * eye[None, :, None, :, None]
        return full.reshape(N_CBLOCKS, LANES, BLOCK_STATE)

    def pack_c(c):
        c4 = c.reshape(N_CBLOCKS, GROUPS_PER_BLOCK, SSM_GROUP, SSM_STATE).transpose(0, 1, 3, 2)
        full = c4[:, :, :, None, :] * eye[None, :, None, :, None]
        return full.reshape(N_CBLOCKS, BLOCK_STATE, LANES)

    return {
        "b": jnp.concatenate([pack_b(bb_re), pack_b(bb_im)], axis=-1).astype(BF16),
        "cre": pack_c(c_re.astype(F32)).astype(BF16),
        "cim": pack_c(-c_im.astype(F32)).astype(BF16),
        "are": jnp.broadcast_to(a_re.reshape(1, STATE_COLS), (batch, STATE_COLS)),
        "aim": jnp.broadcast_to(a_im.reshape(1, STATE_COLS), (batch, STATE_COLS)),
        "d": d_skip.astype(F32).reshape(1, D_MODEL),
        "wglu": w_glu.astype(BF16),
    }


def _kv_prompt_kernel(x_ref, g_ref, w_ref, kp_ref, vtp_ref, k_ref, v_ref):
    i = pl.program_id(1)

    @pl.when(i == 0)
    def _():
        kp_ref[...] = jnp.zeros(kp_ref.shape, BF16)
        vtp_ref[...] = jnp.zeros(vtp_ref.shape, BF16)

    @pl.when(i > 0)
    def _():
        kv = _dot(_rms(x_ref[0], g_ref[...]).astype(BF16), w_ref[...])
        kp_ref[0] = kv[:, :D_MODEL].astype(BF16)
        vtp_ref[0] = kv[:, D_MODEL:].T.astype(BF16)

        @pl.when(i == pl.num_programs(1) - 1)
        def _():
            k_ref[0] = kv[:, :D_MODEL]
            v_ref[0] = kv[:, D_MODEL:]


def _kv_prompt(x, g, w):
    n, l, _ = x.shape
    tiles = l // BAND_PAST
    padded = jax.ShapeDtypeStruct((n, BAND_PAST + l, D_MODEL), BF16)
    padded_t = jax.ShapeDtypeStruct((n, D_MODEL, BAND_PAST + l), BF16)
    tail = jax.ShapeDtypeStruct((n, BAND_PAST, D_MODEL), F32)
    row_block = pl.BlockSpec((1, BAND_PAST, D_MODEL), lambda b, i: (b, i, 0))
    col_block = pl.BlockSpec((1, D_MODEL, BAND_PAST), lambda b, i: (b, 0, i))
    tail_block = pl.BlockSpec((1, BAND_PAST, D_MODEL), lambda b, i: (b, 0, 0))
    return pl.pallas_call(
        _kv_prompt_kernel,
        out_shape=(padded, padded_t, tail, tail),
        grid=(n, tiles + 1),
        in_specs=[
            pl.BlockSpec((1, BAND_PAST, D_MODEL), lambda b, i: (b, jnp.maximum(i - 1, 0), 0)),
            _const_spec(g.shape),
            _const_spec(w.shape),
        ],
        out_specs=(row_block, col_block, tail_block, tail_block),
        compiler_params=pltpu.CompilerParams(
            dimension_semantics=("arbitrary", "arbitrary"), vmem_limit_bytes=VMEM_LIMIT),
        name="kv_prompt",
    )(x, g, w)


def _kv_sample_kernel(x_ref, g_ref, w_ref, k_ref, v_ref):
    kv = _dot(_rms(x_ref[...], g_ref[...]).astype(BF16), w_ref[...])
    k_ref[...] = kv[:, :D_MODEL]
    v_ref[...] = kv[:, D_MODEL:]


def _kv_sample(x, g, w):
    out = jax.ShapeDtypeStruct(x.shape, F32)
    return pl.pallas_call(
        _kv_sample_kernel,
        out_shape=(out, out),
        compiler_params=pltpu.CompilerParams(vmem_limit_bytes=VMEM_LIMIT),
        name="kv_sample",
    )(x, g, w)


def _attn_prompt_kernel(x_ref, g_ref, k_ref, vt_ref, wq_ref, wo_ref, base_ref, o_ref,
                        bias_scr, ot_scr, q_scr, s_scr, p_scr, *, q_rows, k_rows):
    b = pl.program_id(0)
    j = pl.program_id(1)
    width = base_ref.shape[1]
    n_kb = k_rows // KEY_BLOCK
    n_pairs = HEAD_PAIRS

    @pl.when((b == 0) & (j == 0))
    def _():
        kj = lax.broadcasted_iota(jnp.int32, (k_rows, q_rows), 0)
        qi = lax.broadcasted_iota(jnp.int32, (k_rows, q_rows), 1)
        lo = (qi // CHUNK) * CHUNK
        in_band = (kj >= lo) & (kj < lo + BAND)
        for h in range(N_HEADS):
            rows = jnp.broadcast_to(base_ref[h:h + 1, :], (k_rows, width))
            toeplitz = pltpu.roll(rows, 0, 1, stride=1, stride_axis=0)
            bias_scr[h] = jnp.where(in_band, toeplitz[:, :q_rows] * LOG2E, NEG_INF)

    x = x_ref[0]
    xn = _rms(x, g_ref[2:3, :]).astype(BF16)
    q_scr[...] = (_dot(xn, wq_ref[...]) * (SCALE * LOG2E)).astype(BF16)
    row0 = pl.multiple_of(j * q_rows, q_rows)
    lane = lax.broadcasted_iota(jnp.int32, (1, LANES), 1)
    zero = jnp.zeros((), BF16)

    def scores(hp, masked):
        lanes = pl.ds(pl.multiple_of(hp * LANES, LANES), LANES)
        q2 = q_scr[:, lanes]
        maxima = []
        for half in range(2):
            h = 2 * hp + half
            qh = jnp.where((lane < HEAD_DIM) == (half == 0), q2, zero)
            k2 = k_ref[0, pl.ds(row0, k_rows), lanes]
            st = _dot_nt(k2, qh) + bias_scr[h]
            if masked:
                key_row = row0 + lax.broadcasted_iota(jnp.int32, st.shape, 0)
                st = jnp.where(key_row >= BAND_PAST, st, NEG_INF)
            s_scr[half] = st
            maxima.append(jnp.max(st, axis=0, keepdims=True))
        return tuple(maxima)

    def probs(maxima):
        for half in range(2):
            for kb in range(n_kb):
                rows = pl.ds(kb * KEY_BLOCK, KEY_BLOCK)
                p_scr[half, rows, :] = jnp.exp2(s_scr[half, rows, :] - maxima[half]).astype(BF16)

    ones_rows = jnp.ones((2 * SUBLANES, k_rows), BF16)

    def weighted_values(hp):
        for half in range(2):
            h = 2 * hp + half
            rows = pl.ds(pl.multiple_of(h * HEAD_DIM, HEAD_DIM), HEAD_DIM)
            vt = jnp.concatenate([vt_ref[0, rows, pl.ds(row0, k_rows)], ones_rows], axis=0)
            ot = _dot(vt, p_scr[half])
            ot_scr[rows, :] = ot[:HEAD_DIM] * (1.0 / ot[HEAD_DIM:HEAD_DIM + 1])

    def heads(masked):
        m = scores(0, masked)
        probs(m)
        m = scores(1, masked)

        def body(t, m_prev):
            weighted_values(t - 2)
            probs(m_prev)
            return scores(t, masked)

        m = lax.fori_loop(2, n_pairs, body, m)
        weighted_values(n_pairs - 2)
        probs(m)
        weighted_values(n_pairs - 1)

    needs_mask = row0 < BAND_PAST

    @pl.when(needs_mask)
    def _():
        heads(True)

    @pl.when(jnp.logical_not(needs_mask))
    def _():
        heads(False)

    y = _dot(ot_scr[...].T.astype(BF16), wo_ref[...])
    o_ref[0] = x + _rms(y, g_ref[3:4, :])


def _bias_base(table, q_rows, k_rows):
    width = -(-(q_rows + k_rows) // LANES) * LANES
    d = np.zeros((width,), np.int64)
    d[:q_rows] = np.arange(q_rows)
    d[width - (k_rows - 1):] = -np.arange(k_rows - 1, 0, -1)
    idx = np.clip(d + BAND_PAST, -MAX_REL, MAX_REL) + MAX_REL
    return jnp.take(table.astype(F32), jnp.asarray(idx, jnp.int32), axis=1)


def _attn_prompt(x, gains, kp, vtp, wq, wo, table):
    n, l, _ = x.shape
    q_rows = min(ATT_QROWS, l)
    k_rows = q_rows + BAND_PAST
    base = _bias_base(table, q_rows, k_rows)
    return pl.pallas_call(
        functools.partial(_attn_prompt_kernel, q_rows=q_rows, k_rows=k_rows),
        out_shape=jax.ShapeDtypeStruct(x.shape, F32),
        grid=(n, l // q_rows),
        in_specs=[
            pl.BlockSpec((1, q_rows, D_MODEL), lambda b, j: (b, j, 0)),
            _const_spec(gains.shape),
            pl.BlockSpec((1, kp.shape[1], D_MODEL), lambda b, j: (b, 0, 0)),
            pl.BlockSpec((1, D_MODEL, vtp.shape[2]), lambda b, j: (b, 0, 0)),
            _const_spec(wq.shape), _const_spec(wo.shape), _const_spec(base.shape),
        ],
        out_specs=pl.BlockSpec((1, q_rows, D_MODEL), lambda b, j: (b, j, 0)),
        scratch_shapes=[
            pltpu.VMEM((N_HEADS, k_rows, q_rows), F32),
            pltpu.VMEM((D_MODEL, q_rows), F32),
            pltpu.VMEM((q_rows, D_MODEL), BF16),
            pltpu.VMEM((2, k_rows, q_rows), F32),
            pltpu.VMEM((2, k_rows, q_rows), BF16),
        ],
        compiler_params=pltpu.CompilerParams(
            dimension_semantics=("arbitrary", "arbitrary"), vmem_limit_bytes=VMEM_LIMIT),
        name="attn_prompt",
    )(x, gains, kp, vtp, wq, wo, base)


def _attn_sample_kernel(x_ref, g_ref, ck_ref, cv_ref, nk_ref, nv_ref, wq_ref, wo_ref, bc_ref, bn_ref,
                        o_ref, o_scr):
    x = x_ref[0]
    xn = _rms(x, g_ref[2:3, :]).astype(BF16)
    q = (_dot(xn, wq_ref[...]) * SCALE).astype(BF16)
    first_head = lax.broadcasted_iota(jnp.int32, (1, LANES), 1) < HEAD_DIM
    zero = jnp.zeros((), BF16)
    for hp in range(HEAD_PAIRS):
        lanes = slice(hp * LANES, (hp + 1) * LANES)
        q2 = q[:, lanes]
        ck = ck_ref[0, :, lanes].astype(BF16)
        cv = cv_ref[0, :, lanes].astype(BF16)
        nk = nk_ref[0, :, lanes].astype(BF16)
        nv = nv_ref[0, :, lanes].astype(BF16)
        outs = []
        for half, qh in enumerate((jnp.where(first_head, q2, zero), jnp.where(first_head, zero, q2))):
            h = 2 * hp + half
            sc = _dot_nt(qh, ck) + bc_ref[h]
            sn = _dot_nt(qh, nk) + bn_ref[h]
            m = jnp.maximum(jnp.max(sc, axis=-1, keepdims=True), jnp.max(sn, axis=-1, keepdims=True))
            pc = jnp.exp(sc - m)
            pn = jnp.exp(sn - m)
            denom = jnp.sum(pc, axis=-1, keepdims=True) + jnp.sum(pn, axis=-1, keepdims=True)
            outs.append((_dot(pc.astype(BF16), cv) + _dot(pn.astype(BF16), nv)) / denom)
        o_scr[:, lanes] = jnp.where(first_head, outs[0], outs[1])
    y = _dot(o_scr[...].astype(BF16), wo_ref[...])
    o_ref[0] = x + _rms(y, g_ref[3:4, :])


def _attn_sample(x, gains, ck, cv, nk, nv, wq, wo, bias_c, bias_n):
    n, s, _ = x.shape
    r = ck.shape[1]
    seq_block = pl.BlockSpec((1, s, D_MODEL), lambda b: (b, 0, 0))
    cache_block = pl.BlockSpec((1, r, D_MODEL), lambda b: (b, 0, 0))
    return pl.pallas_call(
        _attn_sample_kernel,
        out_shape=jax.ShapeDtypeStruct(x.shape, F32),
        grid=(n,),
        in_specs=[seq_block, _const_spec(gains.shape), cache_block, cache_block, seq_block, seq_block,
                  _const_spec(wq.shape), _const_spec(wo.shape),
                  _const_spec(bias_c.shape), _const_spec(bias_n.shape)],
        out_specs=seq_block,
        scratch_shapes=[pltpu.VMEM((s, D_MODEL), F32)],
        compiler_params=pltpu.CompilerParams(
            dimension_semantics=("arbitrary",), vmem_limit_bytes=VMEM_LIMIT),
        name="attn_sample",
    )(x, gains, ck, cv, nk, nv, wq, wo, bias_c, bias_n)


def _sample_bias(table, s, r):
    q_pos = PAST_LEN + np.arange(s)
    k_pos = np.concatenate([PAST_LEN - r + np.arange(r), q_pos])
    rel = np.clip(q_pos[:, None] - k_pos[None, :], -MAX_REL, MAX_REL) + MAX_REL
    qc = q_pos // CHUNK
    kc = k_pos // CHUNK
    valid = (kc[None, :] <= qc[:, None]) & (kc[None, :] >= qc[:, None] - N_LEFT_CHUNKS)
    bias = jnp.where(valid[None], table.astype(F32)[:, rel], NEG_INF)
    return bias[:, :, :r], bias[:, :, r:]


def kernel(x_prompt, x_sample, state_ssm_re, state_ssm_im, cache_k, cache_v, norm_gains, ffn1_w_in, ffn1_w_out, ffn2_w_in, ffn2_w_out, ssm_lambda_re, ssm_lambda_im, ssm_log_step, ssm_b_re, ssm_b_im, ssm_c_re, ssm_c_im, ssm_d, ssm_w_glu, kv_norm, w_kv, attn_w_q, attn_w_o, attn_rel_bias):
    bp, lp, _ = x_prompt.shape
    bs, ls, _ = x_sample.shape
    cache_rows = cache_k.shape[1]
    assert bp == SUBLANES and bs == SUBLANES and lp % BAND_PAST == 0

    gains = norm_gains.astype(F32)
    kv_g = kv_norm.astype(F32).reshape(1, D_MODEL)
    w_kv_b = w_kv.astype(BF16)
    w1_in, w1_out = ffn1_w_in.astype(F32), ffn1_w_out.astype(F32)
    w2_in, w2_out = ffn2_w_in.astype(F32), ffn2_w_out.astype(F32)
    wq_b = [attn_w_q[i].astype(BF16) for i in range(DEPTH - N_A_LAYERS)]
    wo_b = [attn_w_o[i].astype(BF16) for i in range(DEPTH - N_A_LAYERS)]
    s5 = [_s5_params(ssm_lambda_re[a], ssm_lambda_im[a], ssm_log_step[a], ssm_b_re[a], ssm_b_im[a],
                     ssm_c_re[a], ssm_c_im[a], ssm_d[a], ssm_w_glu[a], SUBLANES)
          for a in range(N_A_LAYERS)]

    xp = x_prompt.astype(F32)
    xs = x_sample.astype(F32)
    zeros = jnp.zeros((bp, STATE_COLS), F32)
    h0_re = state_ssm_re.astype(F32).reshape(N_A_LAYERS, bs, STATE_COLS)
    h0_im = state_ssm_im.astype(F32).reshape(N_A_LAYERS, bs, STATE_COLS)
    states = {"p_re": [], "p_im": [], "s_re": [], "s_im": []}
    for a in range(N_A_LAYERS):
        xp, xs = _half_ffn(xp, xs, gains, w1_in, w1_out, a, 0, 1, "b2t" if a == 0 else None)
        xp, pr, pi = _s5_layer(xp, gains[a], s5[a], zeros, zeros, bp)
        xs, sr, si = _s5_layer(xs, gains[a], s5[a], h0_re[a], h0_im[a], bs)
        xp, xs = _half_ffn(xp, xs, gains, w2_in, w2_out, a, 4, 5, "t2b" if a == N_A_LAYERS - 1 else None)
        for key, val in (("p_re", pr), ("p_im", pi), ("s_re", sr), ("s_im", si)):
            states[key].append(val.reshape(SUBLANES, N_GROUPS, SSM_STATE))

    xs = xs.reshape(bs * ls, D_MODEL)
    kp, vtp, k_tail, v_tail = _kv_prompt(xp, kv_g, w_kv_b)
    k_new, v_new = _kv_sample(xs, kv_g, w_kv_b)
    k_new = k_new.reshape(bs, ls, D_MODEL)
    v_new = v_new.reshape(bs, ls, D_MODEL)
    ck = cache_k.astype(F32).reshape(bs, cache_rows, D_MODEL)
    cv = cache_v.astype(F32).reshape(bs, cache_rows, D_MODEL)
    xp = xp.reshape(bp * lp, D_MODEL)
    for b in range(DEPTH - N_A_LAYERS):
        layer = N_A_LAYERS + b
        xp, xs = _half_ffn(xp, xs, gains, w1_in, w1_out, layer, 0, 1)
        xp = _attn_prompt(xp.reshape(bp, lp, D_MODEL), gains[layer], kp, vtp, wq_b[b], wo_b[b],
                          attn_rel_bias[b]).reshape(bp * lp, D_MODEL)
        bias_c, bias_n = _sample_bias(attn_rel_bias[b], ls, cache_rows)
        xs = _attn_sample(xs.reshape(bs, ls, D_MODEL), gains[layer], ck, cv, k_new, v_new,
                          wq_b[b], wo_b[b], bias_c, bias_n).reshape(bs * ls, D_MODEL)
        xp, xs = _half_ffn(xp, xs, gains, w2_in, w2_out, layer, 4, 5)

    return (xp.reshape(bp, lp, D_MODEL), xs.reshape(bs, ls, D_MODEL),
            jnp.stack(states["p_re"]), jnp.stack(states["p_im"]),
            k_tail.reshape(bp, BAND_PAST, N_HEADS, HEAD_DIM), v_tail.reshape(bp, BAND_PAST, N_HEADS, HEAD_DIM),
            jnp.stack(states["s_re"]), jnp.stack(states["s_im"]),
            k_new.reshape(bs, ls, N_HEADS, HEAD_DIM), v_new.reshape(bs, ls, N_HEADS, HEAD_DIM))
```

```python
import functools
import math

import numpy as np
import jax
import jax.numpy as jnp
from jax import lax
from jax.experimental import pallas as pl
from jax.experimental.pallas import tpu as pltpu

D_MODEL = 1024
DEPTH = 4
PAST_LEN = 4096
CHUNK = 64
N_A_LAYERS = DEPTH // 2
D_FF = 2816
SSM_GROUP = 16
N_GROUPS = D_MODEL // SSM_GROUP
SSM_STATE = 64
N_HEADS = 16
HEAD_DIM = D_MODEL // N_HEADS
N_LEFT_CHUNKS = 8
BAND_PAST = N_LEFT_CHUNKS * CHUNK
BAND = (N_LEFT_CHUNKS + 1) * CHUNK
MAX_REL = 128
EPS = 1e-6
SCALE = HEAD_DIM ** -0.5
NEG_INF = -1e30

F32 = jnp.float32
BF16 = jnp.bfloat16

LANES = 128
SUBLANES = 8
VMEM_LIMIT = 56 * 1024 * 1024

FFN_ROWS = 512
FFN_CHUNK = 256
S5_STEPS = 64
S5_COLS = 1024
GROUPS_PER_BLOCK = LANES // SSM_GROUP
N_CBLOCKS = D_MODEL // LANES
STATE_COLS = N_GROUPS * SSM_STATE
BLOCK_STATE = GROUPS_PER_BLOCK * SSM_STATE
ATT_QROWS = 256
KEY_BLOCK = 128
HEAD_PAIRS = N_HEADS // 2
LOG2E = 1.4426950408889634


def _const_spec(shape):
    zeros = (0,) * len(shape)
    return pl.BlockSpec(shape, lambda *_: zeros, pipeline_mode=pl.Buffered(1))


def _rms(x, g):
    return x * lax.rsqrt(jnp.mean(x * x, axis=-1, keepdims=True) + EPS) * g


def _sigmoid(x):
    return 1.0 / (1.0 + jnp.exp(-x))


def _gelu_tanh(x):
    c = math.sqrt(2.0 / math.pi)
    return 0.5 * x * (1.0 + jnp.tanh(c * (x + 0.044715 * (x * x * x))))


def _dot(a, b):
    return jnp.dot(a, b, preferred_element_type=F32)


def _dot_nt(a, b):
    return lax.dot_general(a, b, (((1,), (1,)), ((), ())), preferred_element_type=F32)


def _ffn_kernel(xp_ref, xs_ref, g_ref, w_in_hbm, w_out_hbm, op_ref, os_ref, w_in_ref, w_out_ref, sems,
                *, layer, pre, post, prompt_tiles, relayout):
    i = pl.program_id(0)

    def weight_copies(c):
        lo = c * FFN_CHUNK
        return (
            pltpu.make_async_copy(w_in_hbm.at[layer, :, pl.ds(lo, FFN_CHUNK)],
                                  w_in_ref.at[:, pl.ds(lo, FFN_CHUNK)], sems.at[0, c]),
            pltpu.make_async_copy(w_in_hbm.at[layer, :, pl.ds(D_FF + lo, FFN_CHUNK)],
                                  w_in_ref.at[:, pl.ds(D_FF + lo, FFN_CHUNK)], sems.at[1, c]),
            pltpu.make_async_copy(w_out_hbm.at[layer, pl.ds(lo, FFN_CHUNK), :],
                                  w_out_ref.at[pl.ds(lo, FFN_CHUNK), :], sems.at[2, c]),
        )

    def half_step(x_ref, o_ref, wait_weights=False):
        x = x_ref[...]
        if relayout == "b2t":
            n, s, _ = x.shape
            x = x.reshape(n * s, D_MODEL)
        xn = _rms(x, g_ref[pre:pre + 1, :]).astype(BF16)
        acc = jnp.zeros(x.shape, F32)
        for c in range(D_FF // FFN_CHUNK):
            lo = c * FFN_CHUNK
            if wait_weights:
                for copy in weight_copies(c):
                    copy.wait()
            gate = _dot(xn, w_in_ref[:, lo:lo + FFN_CHUNK].astype(BF16))
            up = _dot(xn, w_in_ref[:, D_FF + lo:D_FF + lo + FFN_CHUNK].astype(BF16))
            act = (gate * _sigmoid(gate) * up).astype(BF16)
            acc = acc + _dot(act, w_out_ref[lo:lo + FFN_CHUNK, :].astype(BF16))
        out = x + 0.5 * _rms(acc, g_ref[post:post + 1, :])
        if relayout == "b2t":
            out = jnp.swapaxes(out.reshape(n, s, D_MODEL), 0, 1).reshape(n * s, D_MODEL)
        elif relayout == "t2b":
            n, s, _ = o_ref.shape
            out = jnp.swapaxes(out.reshape(s, n, D_MODEL), 0, 1)
        o_ref[...] = out

    @pl.when(i == 0)
    def _():
        for c in range(D_FF // FFN_CHUNK):
            for copy in weight_copies(c):
                copy.start()
        half_step(xp_ref, op_ref, wait_weights=True)

    @pl.when((i > 0) & (i < prompt_tiles))
    def _():
        half_step(xp_ref, op_ref)

    @pl.when(i == prompt_tiles)
    def _():
        half_step(xs_ref, os_ref)


def _layer_spec(w, layer):
    return pl.BlockSpec((None,) + w.shape[1:], lambda *_: (layer,) + (0,) * (w.ndim - 1),
                        pipeline_mode=pl.Buffered(1))


def _half_ffn(xp, xs, gains, w_in, w_out, layer, pre, post, relayout=None):
    batch = SUBLANES
    steps = FFN_ROWS // batch

    def flat(x):
        return (x.shape[0] * x.shape[1], D_MODEL) if x.ndim == 3 else x.shape

    def stacked(x):
        return (batch, x.shape[0] // batch, D_MODEL) if x.ndim == 2 else x.shape

    tiles = flat(xp)[0] // FFN_ROWS
    rows_block = pl.BlockSpec((FFN_ROWS, D_MODEL), lambda i: (jnp.minimum(i, tiles - 1), 0))
    seq_block = pl.BlockSpec((batch, steps, D_MODEL), lambda i: (0, jnp.minimum(i, tiles - 1), 0))
    in_3d, out_3d = relayout == "b2t", relayout == "t2b"
    xs_out = stacked(xs) if out_3d else flat(xs)
    xp_out = stacked(xp) if out_3d else flat(xp)
    return pl.pallas_call(
        functools.partial(_ffn_kernel, layer=layer, pre=pre, post=post, prompt_tiles=tiles, relayout=relayout),
        out_shape=(jax.ShapeDtypeStruct(xp_out, F32), jax.ShapeDtypeStruct(xs_out, F32)),
        grid=(tiles + 1,),
        in_specs=[
            seq_block if in_3d else rows_block,
            pl.BlockSpec(xs.shape, lambda i: (0,) * xs.ndim),
            _layer_spec(gains, layer),
            pl.BlockSpec(memory_space=pl.ANY),
            pl.BlockSpec(memory_space=pl.ANY),
        ],
        out_specs=(seq_block if out_3d else rows_block, pl.BlockSpec(xs_out, lambda i: (0,) * len(xs_out))),
        scratch_shapes=[
            pltpu.VMEM(w_in.shape[1:], F32),
            pltpu.VMEM(w_out.shape[1:], F32),
            pltpu.SemaphoreType.DMA((3, D_FF // FFN_CHUNK)),
        ],
        compiler_params=pltpu.CompilerParams(
            dimension_semantics=("arbitrary",), vmem_limit_bytes=VMEM_LIMIT),
        name="half_ffn",
    )(xp, xs, gains, w_in, w_out)


def _s5_kernel(x_ref, g_ref, b_ref, cre_ref, cim_ref, are_ref, aim_ref,
               h0re_ref, h0im_ref, d_ref, wglu_ref,
               o_ref, hre_out, him_out,
               bure, buim, hre, him, y_scr, *, steps, batch):
    i = pl.program_id(0)

    @pl.when(i == 0)
    def _():
        hre[...] = h0re_ref[...]
        him[...] = h0im_ref[...]

    x = x_ref[...]
    xn = _rms(x, g_ref[2:3, :])
    xb = xn.astype(BF16)

    for k in range(N_CBLOCKS):
        bu = _dot(xb[:, k * LANES:(k + 1) * LANES], b_ref[k])
        bure[:, k * BLOCK_STATE:(k + 1) * BLOCK_STATE] = bu[:, :BLOCK_STATE]
        buim[:, k * BLOCK_STATE:(k + 1) * BLOCK_STATE] = bu[:, BLOCK_STATE:]

    blocks_per_scan = S5_COLS // BLOCK_STATE
    for cb in range(STATE_COLS // S5_COLS):
        cols = slice(cb * S5_COLS, (cb + 1) * S5_COLS)
        ar = are_ref[:, cols]
        ai = aim_ref[:, cols]
        hr = hre[:, cols]
        hi = him[:, cols]
        for t in range(steps):
            rows = slice(t * batch, (t + 1) * batch)
            hr, hi = ar * hr - ai * hi + bure[rows, cols], ar * hi + ai * hr + buim[rows, cols]
            bure[rows, cols] = hr
            buim[rows, cols] = hi
        hre[:, cols] = hr
        him[:, cols] = hi
        hre_out[:, cols] = hr
        him_out[:, cols] = hi
        for k in range(cb * blocks_per_scan, (cb + 1) * blocks_per_scan):
            sc = slice(k * BLOCK_STATE, (k + 1) * BLOCK_STATE)
            yk = _dot(bure[:, sc].astype(BF16), cre_ref[k]) + _dot(buim[:, sc].astype(BF16), cim_ref[k])
            y_scr[:, k * LANES:(k + 1) * LANES] = yk
    y = _gelu_tanh(y_scr[...] + d_ref[...] * xn)
    z = _dot(y.astype(BF16), wglu_ref[...])
    out = z[:, :D_MODEL] * _sigmoid(z[:, D_MODEL:])
    o_ref[...] = x + _rms(out, g_ref[3:4, :])


def _s5_layer(x, gains, p, h0re, h0im, batch):
    rows = x.shape[0]
    steps = min(S5_STEPS, rows // batch)
    tr = steps * batch
    consts = [gains, p["b"], p["cre"], p["cim"], p["are"], p["aim"],
              h0re, h0im, p["d"], p["wglu"]]
    state = jax.ShapeDtypeStruct((batch, STATE_COLS), F32)
    return pl.pallas_call(
        functools.partial(_s5_kernel, steps=steps, batch=batch),
        out_shape=(jax.ShapeDtypeStruct(x.shape, F32), state, state),
        grid=(rows // tr,),
        in_specs=[pl.BlockSpec((tr, D_MODEL), lambda i: (i, 0))] + [_const_spec(c.shape) for c in consts],
        out_specs=(pl.BlockSpec((tr, D_MODEL), lambda i: (i, 0)),
                   pl.BlockSpec((batch, STATE_COLS), lambda i: (0, 0)),
                   pl.BlockSpec((batch, STATE_COLS), lambda i: (0, 0))),
        scratch_shapes=[
            pltpu.VMEM((tr, STATE_COLS), F32),
            pltpu.VMEM((tr, STATE_COLS), F32),
            pltpu.VMEM((batch, STATE_COLS), F32),
            pltpu.VMEM((batch, STATE_COLS), F32),
            pltpu.VMEM((tr, D_MODEL), F32),
        ],
        compiler_params=pltpu.CompilerParams(
            dimension_semantics=("arbitrary",), vmem_limit_bytes=VMEM_LIMIT),
        name="s5_mixer",
    )(x, *consts)


def _s5_params(lam_re, lam_im, log_step, b_re, b_im, c_re, c_im, d_skip, w_glu, batch):
    step = jnp.exp(log_step.astype(F32))[:, None]
    lr = lam_re.astype(F32)
    li = lam_im.astype(F32)
    mag = jnp.exp(lr * step)
    a_re = mag * jnp.cos(li * step)
    a_im = mag * jnp.sin(li * step)
    den = lr * lr + li * li
    coef_re = ((a_re - 1.0) * lr + a_im * li) / den
    coef_im = (a_im * lr - (a_re - 1.0) * li) / den
    bre = b_re.astype(F32)
    bim = b_im.astype(F32)
    bb_re = coef_re[..., None] * bre - coef_im[..., None] * bim
    bb_im = coef_re[..., None] * bim + coef_im[..., None] * bre

    eye = jnp.eye(GROUPS_PER_BLOCK, dtype=F32)

    def pack_b(b):
        b4 = b.reshape(N_CBLOCKS, GROUPS_PER_BLOCK, SSM_STATE, SSM_GROUP).transpose(0, 1, 3, 2)
        full = b4[:, :, :, None, :] * eye[None, :, None, :, None]
        return full.reshape(N_CBLOCKS, LANES, BLOCK_STATE)

    def pack_c(c):
        c4 = c.reshape(N_CBLOCKS, GROUPS_PER_BLOCK, SSM_GROUP, SSM_STATE).transpose(0, 1, 3, 2)
        full = c4[:, :, :, None, :] * eye[None, :, None, :, None]
        return full.reshape(N_CBLOCKS, BLOCK_STATE, LANES)

    return {
        "b": jnp.concatenate([pack_b(bb_re), pack_b(bb_im)], axis=-1).astype(BF16),
        "cre": pack_c(c_re.astype(F32)).astype(BF16),
        "cim": pack_c(-c_im.astype(F32)).astype(BF16),
        "are": jnp.broadcast_to(a_re.reshape(1, STATE_COLS), (batch, STATE_COLS)),
        "aim": jnp.broadcast_to(a_im.reshape(1, STATE_COLS), (batch, STATE_COLS)),
        "d": d_skip.astype(F32).reshape(1, D_MODEL),
        "wglu": w_glu.astype(BF16),
    }


def _kv_prompt_kernel(x_ref, g_ref, w_ref, kp_ref, vtp_ref, k_ref, v_ref):
    i = pl.program_id(1)

    @pl.when(i == 0)
    def _():
        kp_ref[...] = jnp.zeros(kp_ref.shape, BF16)
        vtp_ref[...] = jnp.zeros(vtp_ref.shape, BF16)

    @pl.when(i > 0)
    def _():
        kv = _dot(_rms(x_ref[0], g_ref[...]).astype(BF16), w_ref[...])
        kp_ref[0] = kv[:, :D_MODEL].astype(BF16)
        vtp_ref[0] = kv[:, D_MODEL:].T.astype(BF16)

        @pl.when(i == pl.num_programs(1) - 1)
        def _():
            k_ref[0] = kv[:, :D_MODEL]
            v_ref[0] = kv[:, D_MODEL:]


def _kv_prompt(x, g, w):
    n, l, _ = x.shape
    tiles = l // BAND_PAST
    padded = jax.ShapeDtypeStruct((n, BAND_PAST + l, D_MODEL), BF16)
    padded_t = jax.ShapeDtypeStruct((n, D_MODEL, BAND_PAST + l), BF16)
    tail = jax.ShapeDtypeStruct((n, BAND_PAST, D_MODEL), F32)
    row_block = pl.BlockSpec((1, BAND_PAST, D_MODEL), lambda b, i: (b, i, 0))
    col_block = pl.BlockSpec((1, D_MODEL, BAND_PAST), lambda b, i: (b, 0, i))
    tail_block = pl.BlockSpec((1, BAND_PAST, D_MODEL), lambda b, i: (b, 0, 0))
    return pl.pallas_call(
        _kv_prompt_kernel,
        out_shape=(padded, padded_t, tail, tail),
        grid=(n, tiles + 1),
        in_specs=[
            pl.BlockSpec((1, BAND_PAST, D_MODEL), lambda b, i: (b, jnp.maximum(i - 1, 0), 0)),
            _const_spec(g.shape),
            _const_spec(w.shape),
        ],
        out_specs=(row_block, col_block, tail_block, tail_block),
        compiler_params=pltpu.CompilerParams(
            dimension_semantics=("arbitrary", "arbitrary"), vmem_limit_bytes=VMEM_LIMIT),
        name="kv_prompt",
    )(x, g, w)


def _kv_sample_kernel(x_ref, g_ref, w_ref, k_ref, v_ref):
    kv = _dot(_rms(x_ref[...], g_ref[...]).astype(BF16), w_ref[...])
    k_ref[...] = kv[:, :D_MODEL]
    v_ref[...] = kv[:, D_MODEL:]


def _kv_sample(x, g, w):
    out = jax.ShapeDtypeStruct(x.shape, F32)
    return pl.pallas_call(
        _kv_sample_kernel,
        out_shape=(out, out),
        compiler_params=pltpu.CompilerParams(vmem_limit_bytes=VMEM_LIMIT),
        name="kv_sample",
    )(x, g, w)


def _attn_prompt_kernel(x_ref, g_ref, k_ref, vt_ref, wq_ref, wo_ref, base_ref, o_ref,
                        bias_scr, ot_scr, q_scr, s_scr, p_scr, *, q_rows, k_rows):
    b = pl.program_id(0)
    j = pl.program_id(1)
    width = base_ref.shape[1]
    n_kb = k_rows // KEY_BLOCK
    n_pairs = HEAD_PAIRS

    @pl.when((b == 0) & (j == 0))
    def _():
        kj = lax.broadcasted_iota(jnp.int32, (k_rows, q_rows), 0)
        qi = lax.broadcasted_iota(jnp.int32, (k_rows, q_rows), 1)
        lo = (qi // CHUNK) * CHUNK
        in_band = (kj >= lo) & (kj < lo + BAND)
        for h in range(N_HEADS):
            rows = jnp.broadcast_to(base_ref[h:h + 1, :], (k_rows, width))
            toeplitz = pltpu.roll(rows, 0, 1, stride=1, stride_axis=0)
            bias_scr[h] = jnp.where(in_band, toeplitz[:, :q_rows] * LOG2E, NEG_INF)

    x = x_ref[0]
    xn = _rms(x, g_ref[2:3, :]).astype(BF16)
    q_scr[...] = (_dot(xn, wq_ref[...]) * (SCALE * LOG2E)).astype(BF16)
    row0 = pl.multiple_of(j * q_rows, q_rows)
    lane = lax.broadcasted_iota(jnp.int32, (1, LANES), 1)
    zero = jnp.zeros((), BF16)

    def scores(hp, masked):
        lanes = pl.ds(pl.multiple_of(hp * LANES, LANES), LANES)
        q2 = q_scr[:, lanes]
        maxima = []
        for half in range(2):
            h = 2 * hp + half
            qh = jnp.where((lane < HEAD_DIM) == (half == 0), q2, zero)
            k2 = k_ref[0, pl.ds(row0, k_rows), lanes]
            st = _dot_nt(k2, qh) + bias_scr[h]
            if masked:
                key_row = row0 + lax.broadcasted_iota(jnp.int32, st.shape, 0)
                st = jnp.where(key_row >= BAND_PAST, st, NEG_INF)
            s_scr[half] = st
            maxima.append(jnp.max(st, axis=0, keepdims=True))
        return tuple(maxima)

    def probs(maxima):
        for half in range(2):
            for kb in range(n_kb):
                rows = pl.ds(kb * KEY_BLOCK, KEY_BLOCK)
                p_scr[half, rows, :] = jnp.exp2(s_scr[half, rows, :] - maxima[half]).astype(BF16)

    ones_rows = jnp.ones((2 * SUBLANES, k_rows), BF16)

    def weighted_values(hp):
        for half in range(2):
            h = 2 * hp + half
            rows = pl.ds(pl.multiple_of(h * HEAD_DIM, HEAD_DIM), HEAD_DIM)
            vt = jnp.concatenate([vt_ref[0, rows, pl.ds(row0, k_rows)], ones_rows], axis=0)
            ot = _dot(vt, p_scr[half])
            ot_scr[rows, :] = ot[:HEAD_DIM] * (1.0 / ot[HEAD_DIM:HEAD_DIM + 1])

    def heads(masked):
        m = scores(0, masked)
        probs(m)
        m = scores(1, masked)

        def body(t, m_prev):
            weighted_values(t - 2)
            probs(m_prev)
            return scores(t, masked)

        m = lax.fori_loop(2, n_pairs, body, m)
        weighted_values(n_pairs - 2)
        probs(m)
        weighted_values(n_pairs - 1)

    needs_mask = row0 < BAND_PAST

    @pl.when(needs_mask)
    def _():
        heads(True)

    @pl.when(jnp.logical_not(needs_mask))
    def _():
        heads(False)

    y = _dot(ot_scr[...].T.astype(BF16), wo_ref[...])
    o_ref[0] = x + _rms(y, g_ref[3:4, :])


def _bias_base(table, q_rows, k_rows):
    width = -(-(q_rows + k_rows) // LANES) * LANES
    d = np.zeros((width,), np.int64)
    d[:q_rows] = np.arange(q_rows)
    d[width - (k_rows - 1):] = -np.arange(k_rows - 1, 0, -1)
    idx = np.clip(d + BAND_PAST, -MAX_REL, MAX_REL) + MAX_REL
    return jnp.take(table.astype(F32), jnp.asarray(idx, jnp.int32), axis=1)


def _attn_prompt(x, gains, kp, vtp, wq, wo, table):
    n, l, _ = x.shape
    q_rows = min(ATT_QROWS, l)
    k_rows = q_rows + BAND_PAST
    base = _bias_base(table, q_rows, k_rows)
    return pl.pallas_call(
        functools.partial(_attn_prompt_kernel, q_rows=q_rows, k_rows=k_rows),
        out_shape=jax.ShapeDtypeStruct(x.shape, F32),
        grid=(n, l // q_rows),
        in_specs=[
            pl.BlockSpec((1, q_rows, D_MODEL), lambda b, j: (b, j, 0)),
            _const_spec(gains.shape),
            pl.BlockSpec((1, kp.shape[1], D_MODEL), lambda b, j: (b, 0, 0)),
            pl.BlockSpec((1, D_MODEL, vtp.shape[2]), lambda b, j: (b, 0, 0)),
            _const_spec(wq.shape), _const_spec(wo.shape), _const_spec(base.shape),
        ],
        out_specs=pl.BlockSpec((1, q_rows, D_MODEL), lambda b, j: (b, j, 0)),
        scratch_shapes=[
            pltpu.VMEM((N_HEADS, k_rows, q_rows), F32),
            pltpu.VMEM((D_MODEL, q_rows), F32),
            pltpu.VMEM((q_rows, D_MODEL), BF16),
            pltpu.VMEM((2, k_rows, q_rows), F32),
            pltpu.VMEM((2, k_rows, q_rows), BF16),
        ],
        compiler_params=pltpu.CompilerParams(
            dimension_semantics=("arbitrary", "arbitrary"), vmem_limit_bytes=VMEM_LIMIT),
        name="attn_prompt",
    )(x, gains, kp, vtp, wq, wo, base)


def _attn_sample_kernel(x_ref, g_ref, ck_ref, cv_ref, nk_ref, nv_ref, wq_ref, wo_ref, bc_ref, bn_ref,
                        o_ref, o_scr):
    x = x_ref[0]
    xn = _rms(x, g_ref[2:3, :]).astype(BF16)
    q = (_dot(xn, wq_ref[...]) * SCALE).astype(BF16)
    first_head = lax.broadcasted_iota(jnp.int32, (1, LANES), 1) < HEAD_DIM
    zero = jnp.zeros((), BF16)
    for hp in range(HEAD_PAIRS):
        lanes = slice(hp * LANES, (hp + 1) * LANES)
        q2 = q[:, lanes]
        ck = ck_ref[0, :, lanes].astype(BF16)
        cv = cv_ref[0, :, lanes].astype(BF16)
        nk = nk_ref[0, :, lanes].astype(BF16)
        nv = nv_ref[0, :, lanes].astype(BF16)
        outs = []
        for half, qh in enumerate((jnp.where(first_head, q2, zero), jnp.where(first_head, zero, q2))):
            h = 2 * hp + half
            sc = _dot_nt(qh, ck) + bc_ref[h]
            sn = _dot_nt(qh, nk) + bn_ref[h]
            m = jnp.maximum(jnp.max(sc, axis=-1, keepdims=True), jnp.max(sn, axis=-1, keepdims=True))
            pc = jnp.exp(sc - m)
            pn = jnp.exp(sn - m)
            denom = jnp.sum(pc, axis=-1, keepdims=True) + jnp.sum(pn, axis=-1, keepdims=True)
            outs.append((_dot(pc.astype(BF16), cv) + _dot(pn.astype(BF16), nv)) / denom)
        o_scr[:, lanes] = jnp.where(first_head, outs[0], outs[1])
    y = _dot(o_scr[...].astype(BF16), wo_ref[...])
    o_ref[0] = x + _rms(y, g_ref[3:4, :])


def _attn_sample(x, gains, ck, cv, nk, nv, wq, wo, bias_c, bias_n):
    n, s, _ = x.shape
    r = ck.shape[1]
    seq_block = pl.BlockSpec((1, s, D_MODEL), lambda b: (b, 0, 0))
    cache_block = pl.BlockSpec((1, r, D_MODEL), lambda b: (b, 0, 0))
    return pl.pallas_call(
        _attn_sample_kernel,
        out_shape=jax.ShapeDtypeStruct(x.shape, F32),
        grid=(n,),
        in_specs=[seq_block, _const_spec(gains.shape), cache_block, cache_block, seq_block, seq_block,
                  _const_spec(wq.shape), _const_spec(wo.shape),
                  _const_spec(bias_c.shape), _const_spec(bias_n.shape)],
        out_specs=seq_block,
        scratch_shapes=[pltpu.VMEM((s, D_MODEL), F32)],
        compiler_params=pltpu.CompilerParams(
            dimension_semantics=("arbitrary",), vmem_limit_bytes=VMEM_LIMIT),
        name="attn_sample",
    )(x, gains, ck, cv, nk, nv, wq, wo, bias_c, bias_n)


def _sample_bias(table, s, r):
    q_pos = PAST_LEN + np.arange(s)
    k_pos = np.concatenate([PAST_LEN - r + np.arange(r), q_pos])
    rel = np.clip(q_pos[:, None] - k_pos[None, :], -MAX_REL, MAX_REL) + MAX_REL
    qc = q_pos // CHUNK
    kc = k_pos // CHUNK
    valid = (kc[None, :] <= qc[:, None]) & (kc[None, :] >= qc[:, None] - N_LEFT_CHUNKS)
    bias = jnp.where(valid[None], table.astype(F32)[:, rel], NEG_INF)
    return bias[:, :, :r], bias[:, :, r:]


def kernel(x_prompt, x_sample, state_ssm_re, state_ssm_im, cache_k, cache_v, norm_gains, ffn1_w_in, ffn1_w_out, ffn2_w_in, ffn2_w_out, ssm_lambda_re, ssm_lambda_im, ssm_log_step, ssm_b_re, ssm_b_im, ssm_c_re, ssm_c_im, ssm_d, ssm_w_glu, kv_norm, w_kv, attn_w_q, attn_w_o, attn_rel_bias):
    bp, lp, _ = x_prompt.shape
    bs, ls, _ = x_sample.shape
    cache_rows = cache_k.shape[1]
    assert bp == SUBLANES and bs == SUBLANES and lp % BAND_PAST == 0

    gains = norm_gains.astype(F32)
    kv_g = kv_norm.astype(F32).reshape(1, D_MODEL)
    w_kv_b = w_kv.astype(BF16)
    w1_in, w1_out = ffn1_w_in.astype(F32), ffn1_w_out.astype(F32)
    w2_in, w2_out = ffn2_w_in.astype(F32), ffn2_w_out.astype(F32)
    wq_b = [attn_w_q[i].astype(BF16) for i in range(DEPTH - N_A_LAYERS)]
    wo_b = [attn_w_o[i].astype(BF16) for i in range(DEPTH - N_A_LAYERS)]
    s5 = [_s5_params(ssm_lambda_re[a], ssm_lambda_im[a], ssm_log_step[a], ssm_b_re[a], ssm_b_im[a],
                     ssm_c_re[a], ssm_c_im[a], ssm_d[a], ssm_w_glu[a], SUBLANES)
          for a in range(N_A_LAYERS)]

    xp = x_prompt.astype(F32)
    xs = x_sample.astype(F32)
    zeros = jnp.zeros((bp, STATE_COLS), F32)
    h0_re = state_ssm_re.astype(F32).reshape(N_A_LAYERS, bs, STATE_COLS)
    h0_im = state_ssm_im.astype(F32).reshape(N_A_LAYERS, bs, STATE_COLS)
    states = {"p_re": [], "p_im": [], "s_re": [], "s_im": []}
    for a in range(N_A_LAYERS):
        xp, xs = _half_ffn(xp, xs, gains, w1_in, w1_out, a, 0, 1, "b2t" if a == 0 else None)
        xp, pr, pi = _s5_layer(xp, gains[a], s5[a], zeros, zeros, bp)
        xs, sr, si = _s5_layer(xs, gains[a], s5[a], h0_re[a], h0_im[a], bs)
        xp, xs = _half_ffn(xp, xs, gains, w2_in, w2_out, a, 4, 5, "t2b" if a == N_A_LAYERS - 1 else None)
        for key, val in (("p_re", pr), ("p_im", pi), ("s_re", sr), ("s_im", si)):
            states[key].append(val.reshape(SUBLANES, N_GROUPS, SSM_STATE))

    xs = xs.reshape(bs * ls, D_MODEL)
    kp, vtp, k_tail, v_tail = _kv_prompt(xp, kv_g, w_kv_b)
    k_new, v_new = _kv_sample(xs, kv_g, w_kv_b)
    k_new = k_new.reshape(bs, ls, D_MODEL)
    v_new = v_new.reshape(bs, ls, D_MODEL)
    ck = cache_k.astype(F32).reshape(bs, cache_rows, D_MODEL)
    cv = cache_v.astype(F32).reshape(bs, cache_rows, D_MODEL)
    xp = xp.reshape(bp * lp, D_MODEL)
    for b in range(DEPTH - N_A_LAYERS):
        layer = N_A_LAYERS + b
        xp, xs = _half_ffn(xp, xs, gains, w1_in, w1_out, layer, 0, 1)
        xp = _attn_prompt(xp.reshape(bp, lp, D_MODEL), gains[layer], kp, vtp, wq_b[b], wo_b[b],
                          attn_rel_bias[b]).reshape(bp * lp, D_MODEL)
        bias_c, bias_n = _sample_bias(attn_rel_bias[b], ls, cache_rows)
        xs = _attn_sample(xs.reshape(bs, ls, D_MODEL), gains[layer], ck, cv, k_new, v_new,
                          wq_b[b], wo_b[b], bias_c, bias_n).reshape(bs * ls, D_MODEL)
        xp, xs = _half_ffn(xp, xs, gains, w2_in, w2_out, layer, 4, 5)

    return (xp.reshape(bp, lp, D_MODEL), xs.reshape(bs, ls, D_MODEL),
            jnp.stack(states["p_re"]), jnp.stack(states["p_im"]),
            k_tail.reshape(bp, BAND_PAST, N_HEADS, HEAD_DIM), v_tail.reshape(bp, BAND_PAST, N_HEADS, HEAD_DIM),
            jnp.stack(states["s_re"]), jnp.stack(states["s_im"]),
            k_new.reshape(bs, ls, N_HEADS, HEAD_DIM), v_new.reshape(bs, ls, N_HEADS, HEAD_DIM))
```
